```python
import math
import jax, jax.numpy as jnp
from jax import lax
import numpy as np

D_MODEL = 1024
BATCH = 4
SEQ = 4096
DEPTH = 2

N_A = DEPTH // 2
N_B = DEPTH - N_A

D_FF = ((8 * D_MODEL // 3 + 127) // 128) * 128
FFN_RES_SCALE = 0.5

D_A = 2 * D_MODEL
CHUNK = 128
SGU_GROUPS = 8
SGU_GROUP_DIM = D_A // SGU_GROUPS

SB_HEADS = 16
SB_HEAD_DIM = D_MODEL // SB_HEADS
SB_KV_HEADS = 4
SB_GROUP = SB_HEADS // SB_KV_HEADS
Q_BLOCK = 128

NORM_EPS = 1e-6
LN_EPS = 1e-5

kernel_name = "yoco_gmlp_stickbreak_macaron"


def rms_norm(x, g):
    xf = x.astype(jnp.float32)
    y = xf * lax.rsqrt(jnp.mean(xf * xf, axis=-1, keepdims=True) + NORM_EPS)
    return (y * g.astype(jnp.float32)).astype(x.dtype)


def layer_norm(x, g, b):
    xf = x.astype(jnp.float32)
    mu = jnp.mean(xf, axis=-1, keepdims=True)
    xc = xf - mu
    var = jnp.mean(xc * xc, axis=-1, keepdims=True)
    y = xc * lax.rsqrt(var + LN_EPS) * g.astype(jnp.float32) + b.astype(jnp.float32)
    return y.astype(x.dtype)


def swiglu(h, w_in, w_out):
    gu = h @ w_in
    gate, up = jnp.split(gu, 2, axis=-1)
    return (jax.nn.silu(gate) * up) @ w_out


def chunked_sgu(h, w_in, ln_g, ln_b, w_s, b_s, w_out):
    B, S, _ = h.shape
    z = jax.nn.gelu(h @ w_in)
    u, v = jnp.split(z, 2, axis=-1)
    v = layer_norm(v, ln_g, ln_b)
    v = v.reshape(B, S // CHUNK, CHUNK, SGU_GROUPS, SGU_GROUP_DIM)
    causal = jnp.tril(jnp.ones((CHUNK, CHUNK), dtype=w_s.dtype))
    ws = w_s * causal[None]
    v = jnp.einsum('gts,bnsgc->bntgc', ws, v) + b_s.T[None, None, :, :, None]
    v = v.reshape(B, S, D_A)
    return (u * v) @ w_out


def stick_breaking_attention(q, k, v):
    B, S = q.shape[0], q.shape[1]
    nb = S // Q_BLOCK
    scale = 1.0 / math.sqrt(SB_HEAD_DIM)
    qf = (q.astype(jnp.float32) * scale).reshape(B, nb, Q_BLOCK, SB_KV_HEADS, SB_GROUP, SB_HEAD_DIM)
    qf = jnp.moveaxis(qf, 1, 0)
    kf = k.astype(jnp.float32)
    vf = v.astype(jnp.float32)
    key_pos = jnp.arange(S)

    def one_block(args):
        qb, bi = args
        q_pos = bi * Q_BLOCK + jnp.arange(Q_BLOCK)
        mask = key_pos[None, :] < q_pos[:, None]
        z = jnp.einsum('bqkgd,bskd->bkgqs', qb, kf)
        log_1m_beta = jnp.where(mask, jax.nn.log_sigmoid(-z), 0.0)
        rev = lax.cumsum(log_1m_beta, axis=4, reverse=True)
        excl = jnp.concatenate([rev[..., 1:], jnp.zeros_like(rev[..., :1])], axis=-1)
        a = jnp.where(mask, jnp.exp(jax.nn.log_sigmoid(z) + excl), 0.0)
        o = jnp.einsum('bkgqs,bskd->bqkgd', a, vf)
        return o.reshape(B, Q_BLOCK, SB_HEADS * SB_HEAD_DIM)

    o = lax.map(one_block, (qf, jnp.arange(nb)))
    o = jnp.moveaxis(o, 0, 1).reshape(B, S, SB_HEADS * SB_HEAD_DIM)
    return o.astype(q.dtype)


def setup_inputs(seed: int = 0) -> dict:
    key = jax.random.key(seed)
    ks = jax.random.split(key, 20)
    f32 = jnp.float32
    nrm = lambda k, shape, fan_in: jax.random.normal(k, shape, f32) * (fan_in ** -0.5)
    gain = lambda k, shape: 1.0 + 0.02 * jax.random.normal(k, shape, f32)
    HD = SB_HEADS * SB_HEAD_DIM
    KVD = SB_KV_HEADS * SB_HEAD_DIM
    return {
        "x": jax.random.normal(ks[0], (BATCH, SEQ, D_MODEL), f32),
        "ffn_norm": gain(ks[1], (DEPTH, 2, D_MODEL)),
        "ffn_w_in": nrm(ks[2], (DEPTH, 2, D_MODEL, 2 * D_FF), D_MODEL),
        "ffn_w_out": nrm(ks[3], (DEPTH, 2, D_FF, D_MODEL), D_FF),
        "mix_norm": gain(ks[4], (DEPTH, D_MODEL)),
        "a_w_in": nrm(ks[5], (N_A, D_MODEL, 2 * D_A), D_MODEL),
        "a_ln_g": gain(ks[6], (N_A, D_A)),
        "a_ln_b": 0.02 * jax.random.normal(ks[7], (N_A, D_A), f32),
        "a_w_s": 0.5 * nrm(ks[8], (N_A, SGU_GROUPS, CHUNK, CHUNK), CHUNK),
        "a_b_s": gain(ks[9], (N_A, SGU_GROUPS, CHUNK)),
        "a_w_out": nrm(ks[10], (N_A, D_A, D_MODEL), D_A),
        "kv_norm": gain(ks[11], (D_MODEL,)),
        "w_kv": nrm(ks[12], (D_MODEL, 2 * KVD), D_MODEL),
        "b_w_q": nrm(ks[13], (N_B, D_MODEL, HD), D_MODEL),
        "b_w_o": nrm(ks[14], (N_B, HD, D_MODEL), HD),
        "final_norm": gain(ks[15], (D_MODEL,)),
    }


def reference(x, ffn_norm, ffn_w_in, ffn_w_out, mix_norm, a_w_in, a_ln_g, a_ln_b, a_w_s,
              a_b_s, a_w_out, kv_norm, w_kv, b_w_q, b_w_o, final_norm):
    B, S, _ = x.shape
    k_shared = None
    v_shared = None
    for i in range(DEPTH):
        if i == N_A:
            kv = rms_norm(x, kv_norm) @ w_kv
            k_flat, v_flat = jnp.split(kv, 2, axis=-1)
            k_shared = k_flat.reshape(B, S, SB_KV_HEADS, SB_HEAD_DIM)
            v_shared = v_flat.reshape(B, S, SB_KV_HEADS, SB_HEAD_DIM)
        x = x + FFN_RES_SCALE * swiglu(rms_norm(x, ffn_norm[i, 0]), ffn_w_in[i, 0], ffn_w_out[i, 0])
        h = rms_norm(x, mix_norm[i])
        if i < N_A:
            x = x + chunked_sgu(h, a_w_in[i], a_ln_g[i], a_ln_b[i], a_w_s[i], a_b_s[i], a_w_out[i])
        else:
            j = i - N_A
            q = (h @ b_w_q[j]).reshape(B, S, SB_HEADS, SB_HEAD_DIM)
            x = x + stick_breaking_attention(q, k_shared, v_shared) @ b_w_o[j]
        x = x + FFN_RES_SCALE * swiglu(rms_norm(x, ffn_norm[i, 1]), ffn_w_in[i, 1], ffn_w_out[i, 1])
    return rms_norm(x, final_norm)
```

```python
import functools
import math

import jax
import jax.numpy as jnp
from jax import lax
from jax.experimental import pallas as pl
from jax.experimental.pallas import tpu as pltpu

D_MODEL = 1024
D_FF = 2816
D_A = 2 * D_MODEL
CHUNK = 128
SGU_GROUPS = 8
SGU_GROUP_DIM = D_A // SGU_GROUPS
SB_HEADS = 16
SB_HEAD_DIM = 64
SB_KV_HEADS = 4
SB_GROUP = SB_HEADS // SB_KV_HEADS
KV_DIM = SB_KV_HEADS * SB_HEAD_DIM
FFN_RES_SCALE = 0.5
NORM_EPS = 1e-6
LN_EPS = 1e-5
LOG2E = 1.4426950408889634

VMEM_LIMIT_BYTES = 56 * 1024 * 1024

TOKEN_TILE = 512
FFN_CHUNK = 256
ATT_BLOCK = 256

BF16 = jnp.bfloat16
F32 = jnp.float32


def _const_spec(shape):
    nd = len(shape)
    return pl.BlockSpec(shape, lambda *_: (0,) * nd, pipeline_mode=pl.Buffered(1))


def _rms_norm(x, g):
    return x * lax.rsqrt(jnp.mean(x * x, axis=-1, keepdims=True) + NORM_EPS) * g


def _params(n_grid_dims):
    return pltpu.CompilerParams(
        dimension_semantics=("parallel",) * n_grid_dims,
        vmem_limit_bytes=VMEM_LIMIT_BYTES)


def _ffn_kernel(x_ref, g_ref, win_ref, wout_ref, *rest, final):
    if final:
        fg_ref, o_ref = rest
    else:
        (o_ref,) = rest
    x = x_ref[...]
    h = _rms_norm(x, g_ref[...]).astype(BF16)
    acc = jnp.zeros(x.shape, F32)
    for c in range(D_FF // FFN_CHUNK):
        lo = c * FFN_CHUNK
        gate = jnp.dot(h, win_ref[:, lo:lo + FFN_CHUNK], preferred_element_type=F32)
        up = jnp.dot(h, win_ref[:, D_FF + lo:D_FF + lo + FFN_CHUNK], preferred_element_type=F32)
        act = (gate * jax.nn.sigmoid(gate) * up).astype(BF16)
        acc = acc + jnp.dot(act, wout_ref[lo:lo + FFN_CHUNK, :], preferred_element_type=F32)
    y = x + FFN_RES_SCALE * acc
    if final:
        y = _rms_norm(y, fg_ref[...])
    o_ref[...] = y


def _ffn(x2, g, w_in, w_out, final_g=None):
    T = x2.shape[0]
    final = final_g is not None
    row = pl.BlockSpec((TOKEN_TILE, D_MODEL), lambda i: (i, 0))
    in_specs = [row, _const_spec((1, D_MODEL)), _const_spec(w_in.shape), _const_spec(w_out.shape)]
    args = [x2, g.reshape(1, D_MODEL), w_in, w_out]
    if final:
        in_specs.append(_const_spec((1, D_MODEL)))
        args.append(final_g.reshape(1, D_MODEL))
    return pl.pallas_call(
        functools.partial(_ffn_kernel, final=final),
        grid=(T // TOKEN_TILE,),
        in_specs=in_specs,
        out_specs=row,
        out_shape=jax.ShapeDtypeStruct((T, D_MODEL), F32),
        compiler_params=_params(1),
        name="ffn_final" if final else "ffn",
    )(*args)


def _sgu_kernel(x_ref, g_ref, win_ref, lng_ref, lnb_ref, ws_ref, bs_ref, wout_ref, o_ref):
    x = x_ref[...]
    tm = x.shape[0]
    h = _rms_norm(x, g_ref[...]).astype(BF16)
    u = jax.nn.gelu(jnp.dot(h, win_ref[:, :D_A], preferred_element_type=F32))
    v = jax.nn.gelu(jnp.dot(h, win_ref[:, D_A:], preferred_element_type=F32))
    mu = jnp.mean(v, axis=-1, keepdims=True)
    vc = v - mu
    var = jnp.mean(vc * vc, axis=-1, keepdims=True)
    vn = (vc * lax.rsqrt(var + LN_EPS) * lng_ref[...] + lnb_ref[...]).astype(BF16)

    row = lax.broadcasted_iota(jnp.int32, (CHUNK, CHUNK), 0)
    col = lax.broadcasted_iota(jnp.int32, (CHUNK, CHUNK), 1)
    causal = row >= col
    bs = bs_ref[...]
    group_cols = []
    for gi in range(SGU_GROUPS):
        ws = jnp.where(causal, ws_ref[gi], 0.0).astype(BF16)
        bias = bs[:, gi:gi + 1]
        c0 = gi * SGU_GROUP_DIM
        chunk_rows = []
        for n in range(tm // CHUNK):
            vb = vn[n * CHUNK:(n + 1) * CHUNK, c0:c0 + SGU_GROUP_DIM]
            chunk_rows.append(jnp.dot(ws, vb, preferred_element_type=F32) + bias)
        group_cols.append(jnp.concatenate(chunk_rows, axis=0))
    vs = jnp.concatenate(group_cols, axis=1)
    gated = (u * vs).astype(BF16)
    o_ref[...] = x + jnp.dot(gated, wout_ref[...], preferred_element_type=F32)


def _sgu(x2, g, w_in, ln_g, ln_b, w_s, b_s, w_out):
    T = x2.shape[0]
    tm = 256
    row = pl.BlockSpec((tm, D_MODEL), lambda i: (i, 0))
    return pl.pallas_call(
        _sgu_kernel,
        grid=(T // tm,),
        in_specs=[row, _const_spec((1, D_MODEL)), _const_spec(w_in.shape),
                  _const_spec((1, D_A)), _const_spec((1, D_A)),
                  _const_spec(w_s.shape), _const_spec((CHUNK, SGU_GROUPS)),
                  _const_spec(w_out.shape)],
        out_specs=row,
        out_shape=jax.ShapeDtypeStruct((T, D_MODEL), F32),
        compiler_params=_params(1),
        name="sgu",
    )(x2, g.reshape(1, D_MODEL), w_in, ln_g.reshape(1, D_A), ln_b.reshape(1, D_A),
      w_s, b_s.T, w_out)


def _norm_proj_kernel(x_ref, g_ref, w_ref, *o_refs, scale):
    h = _rms_norm(x_ref[0], g_ref[...]).astype(BF16)
    y = jnp.dot(h, w_ref[...], preferred_element_type=F32)
    if scale != 1.0:
        y = y * scale
    y = y.astype(BF16)
    heads_per_out = o_refs[0].shape[1]
    for oi, o_ref in enumerate(o_refs):
        for hd in range(heads_per_out):
            c0 = (oi * heads_per_out + hd) * SB_HEAD_DIM
            o_ref[0, hd] = y[:, c0:c0 + SB_HEAD_DIM]


def _norm_proj(x3, g, w, n_out, scale):
    B, S, _ = x3.shape
    heads = w.shape[1] // (n_out * SB_HEAD_DIM)
    tm = TOKEN_TILE
    out_spec = pl.BlockSpec((1, heads, tm, SB_HEAD_DIM), lambda b, i: (b, 0, i, 0))
    out_shape = jax.ShapeDtypeStruct((B, heads, S, SB_HEAD_DIM), BF16)
    return pl.pallas_call(
        functools.partial(_norm_proj_kernel, scale=scale),
        grid=(B, S // tm),
        in_specs=[pl.BlockSpec((1, tm, D_MODEL), lambda b, i: (b, i, 0)),
                  _const_spec((1, D_MODEL)), _const_spec(w.shape)],
        out_specs=[out_spec] * n_out,
        out_shape=[out_shape] * n_out,
        compiler_params=_params(2),
        name="norm_proj",
    )(x3, g.reshape(1, D_MODEL), w)


def _attn_kernel(q_ref, k_ref, v_ref, o_ref, acc_ref):
    qi = pl.program_id(2)
    blk = ATT_BLOCK
    row = lax.broadcasted_iota(jnp.int32, (blk, blk), 0)
    col = lax.broadcasted_iota(jnp.int32, (blk, blk), 1)
    suffix = jnp.where(row > col, -1.0, 0.0).astype(BF16)
    strictly_earlier = col < row

    def block(j, run, diag):
        kb = k_ref[0, 0, pl.ds(pl.multiple_of(j * blk, blk), blk), :]
        vb = v_ref[0, 0, pl.ds(pl.multiple_of(j * blk, blk), blk), :]
        new_run = []
        for hd in range(SB_GROUP):
            z = lax.dot_general(q_ref[0, hd], kb, (((1,), (1,)), ((), ())),
                                preferred_element_type=F32)
            sp = jnp.maximum(z, 0.0) + jnp.log2(1.0 + jnp.exp2(-jnp.abs(z)))
            if diag:
                sp = jnp.where(strictly_earlier, sp, 0.0)
            hi = sp.astype(BF16)
            lo = (sp - hi.astype(F32)).astype(BF16)
            excl = (jnp.dot(hi, suffix, preferred_element_type=F32)
                    + jnp.dot(lo, suffix, preferred_element_type=F32))
            a = jnp.exp2(z - sp + excl + run[hd])
            if diag:
                a = jnp.where(strictly_earlier, a, 0.0)
            pv = jnp.dot(a.astype(BF16), vb, preferred_element_type=F32)
            if diag:
                acc_ref[hd] = pv
            else:
                acc_ref[hd] += pv
            new_run.append(run[hd] - jnp.sum(sp, axis=-1, keepdims=True))
        return tuple(new_run)

    run0 = tuple(jnp.zeros((blk, 1), F32) for _ in range(SB_GROUP))
    run = block(qi, run0, True)
    lax.fori_loop(0, qi, lambda i, r: block(qi - 1 - i, r, False), run)
    o_ref[0] = jnp.concatenate([acc_ref[hd] for hd in range(SB_GROUP)], axis=1).astype(o_ref.dtype)


def _attention(q, k, v):
    B, _, S, _ = q.shape
    blk = ATT_BLOCK
    kv_spec = pl.BlockSpec((1, 1, S, SB_HEAD_DIM), lambda b, kh, i: (b, kh, 0, 0))
    return pl.pallas_call(
        _attn_kernel,
        grid=(B, SB_KV_HEADS, S // blk),
        in_specs=[pl.BlockSpec((1, SB_GROUP, blk, SB_HEAD_DIM), lambda b, kh, i: (b, kh, i, 0)),
                  kv_spec, kv_spec],
        out_specs=pl.BlockSpec((1, blk, SB_GROUP * SB_HEAD_DIM), lambda b, kh, i: (b, i, kh)),
        out_shape=jax.ShapeDtypeStruct((B, S, SB_HEADS * SB_HEAD_DIM), BF16),
        scratch_shapes=[pltpu.VMEM((SB_GROUP, blk, SB_HEAD_DIM), F32)],
        compiler_params=_params(3),
        name="sb_attention",
    )(q, k, v)


def _proj_res_kernel(o_ref, w_ref, x_ref, out_ref):
    out_ref[...] = x_ref[...] + jnp.dot(o_ref[...], w_ref[...], preferred_element_type=F32)


def _proj_res(o2, w, x2):
    T = x2.shape[0]
    row = pl.BlockSpec((TOKEN_TILE, D_MODEL), lambda i: (i, 0))
    return pl.pallas_call(
        _proj_res_kernel,
        grid=(T // TOKEN_TILE,),
        in_specs=[row, _const_spec(w.shape), row],
        out_specs=row,
        out_shape=jax.ShapeDtypeStruct((T, D_MODEL), F32),
        compiler_params=_params(1),
        name="proj_res",
    )(o2, w, x2)


def kernel(x, ffn_norm, ffn_w_in, ffn_w_out, mix_norm, a_w_in, a_ln_g, a_ln_b, a_w_s, a_b_s,
           a_w_out, kv_norm, w_kv, b_w_q, b_w_o, final_norm):
    B, S, D = x.shape
    depth = ffn_norm.shape[0]
    n_a = a_w_in.shape[0]
    w_in = ffn_w_in.astype(BF16)
    w_out = ffn_w_out.astype(BF16)
    x2 = x.reshape(B * S, D)
    k = v = None
    for i in range(depth):
        if i == n_a:
            k, v = _norm_proj(x2.reshape(B, S, D), kv_norm, w_kv.astype(BF16), 2, 1.0)
        x2 = _ffn(x2, ffn_norm[i, 0], w_in[i, 0], w_out[i, 0])
        if i < n_a:
            x2 = _sgu(x2, mix_norm[i], a_w_in[i].astype(BF16), a_ln_g[i], a_ln_b[i],
                      a_w_s[i], a_b_s[i], a_w_out[i].astype(BF16))
        else:
            j = i - n_a
            (q,) = _norm_proj(x2.reshape(B, S, D), mix_norm[i], b_w_q[j].astype(BF16), 1,
                              LOG2E / math.sqrt(SB_HEAD_DIM))
            o = _attention(q, k, v)
            x2 = _proj_res(o.reshape(B * S, D), b_w_o[j].astype(BF16), x2)
        x2 = _ffn(x2, ffn_norm[i, 1], w_in[i, 1], w_out[i, 1],
                  final_g=final_norm if i == depth - 1 else None)
    return x2.reshape(B, S, D)
```

```python
import functools
import math

import jax
import jax.numpy as jnp
from jax import lax
from jax.experimental import pallas as pl
from jax.experimental.pallas import tpu as pltpu

D_MODEL = 1024
D_FF = 2816
D_A = 2 * D_MODEL
CHUNK = 128
SGU_GROUPS = 8
SGU_GROUP_DIM = D_A // SGU_GROUPS
SB_HEADS = 16
SB_HEAD_DIM = 64
SB_KV_HEADS = 4
SB_GROUP = SB_HEADS // SB_KV_HEADS
KV_DIM = SB_KV_HEADS * SB_HEAD_DIM
FFN_RES_SCALE = 0.5
NORM_EPS = 1e-6
LN_EPS = 1e-5
LOG2E = 1.4426950408889634

VMEM_LIMIT_BYTES = 56 * 1024 * 1024

TOKEN_TILE = 512
FFN_CHUNK = 256
ATT_BLOCK = 256
ATT_Q_SUB = 4

BF16 = jnp.bfloat16
F32 = jnp.float32


def _const_spec(shape):
    nd = len(shape)
    return pl.BlockSpec(shape, lambda *_: (0,) * nd, pipeline_mode=pl.Buffered(1))


def _rms_norm(x, g):
    return x * lax.rsqrt(jnp.mean(x * x, axis=-1, keepdims=True) + NORM_EPS) * g


def _params(n_grid_dims):
    return pltpu.CompilerParams(
        dimension_semantics=("parallel",) * n_grid_dims,
        vmem_limit_bytes=VMEM_LIMIT_BYTES)


def _ffn_kernel(x_ref, g_ref, win_ref, wout_ref, *rest, final):
    if final:
        fg_ref, o_ref = rest
    else:
        (o_ref,) = rest
    x = x_ref[...]
    h = _rms_norm(x, g_ref[...]).astype(BF16)
    acc = jnp.zeros(x.shape, F32)
    for c in range(D_FF // FFN_CHUNK):
        lo = c * FFN_CHUNK
        gate = jnp.dot(h, win_ref[:, lo:lo + FFN_CHUNK], preferred_element_type=F32)
        up = jnp.dot(h, win_ref[:, D_FF + lo:D_FF + lo + FFN_CHUNK], preferred_element_type=F32)
        act = (gate * jax.nn.sigmoid(gate) * up).astype(BF16)
        acc = acc + jnp.dot(act, wout_ref[lo:lo + FFN_CHUNK, :], preferred_element_type=F32)
    y = x + FFN_RES_SCALE * acc
    if final:
        y = _rms_norm(y, fg_ref[...])
    o_ref[...] = y


def _ffn(x2, g, w_in, w_out, final_g=None):
    T = x2.shape[0]
    final = final_g is not None
    row = pl.BlockSpec((TOKEN_TILE, D_MODEL), lambda i: (i, 0))
    in_specs = [row, _const_spec((1, D_MODEL)), _const_spec(w_in.shape), _const_spec(w_out.shape)]
    args = [x2, g.reshape(1, D_MODEL), w_in, w_out]
    if final:
        in_specs.append(_const_spec((1, D_MODEL)))
        args.append(final_g.reshape(1, D_MODEL))
    return pl.pallas_call(
        functools.partial(_ffn_kernel, final=final),
        grid=(T // TOKEN_TILE,),
        in_specs=in_specs,
        out_specs=row,
        out_shape=jax.ShapeDtypeStruct((T, D_MODEL), F32),
        compiler_params=_params(1),
        name="ffn_final" if final else "ffn",
    )(*args)


def _sgu_kernel(x_ref, g_ref, win_ref, lng_ref, lnb_ref, ws_ref, bs_ref, wout_ref, o_ref):
    x = x_ref[...]
    tm = x.shape[0]
    h = _rms_norm(x, g_ref[...]).astype(BF16)
    u = jax.nn.gelu(jnp.dot(h, win_ref[:, :D_A], preferred_element_type=F32))
    v = jax.nn.gelu(jnp.dot(h, win_ref[:, D_A:], preferred_element_type=F32))
    mu = jnp.mean(v, axis=-1, keepdims=True)
    vc = v - mu
    var = jnp.mean(vc * vc, axis=-1, keepdims=True)
    vn = (vc * lax.rsqrt(var + LN_EPS) * lng_ref[...] + lnb_ref[...]).astype(BF16)

    row = lax.broadcasted_iota(jnp.int32, (CHUNK, CHUNK), 0)
    col = lax.broadcasted_iota(jnp.int32, (CHUNK, CHUNK), 1)
    causal = row >= col
    bs = bs_ref[...]
    group_cols = []
    for gi in range(SGU_GROUPS):
        ws = jnp.where(causal, ws_ref[gi], 0.0).astype(BF16)
        bias = bs[:, gi:gi + 1]
        c0 = gi * SGU_GROUP_DIM
        chunk_rows = []
        for n in range(tm // CHUNK):
            vb = vn[n * CHUNK:(n + 1) * CHUNK, c0:c0 + SGU_GROUP_DIM]
            chunk_rows.append(jnp.dot(ws, vb, preferred_element_type=F32) + bias)
        group_cols.append(jnp.concatenate(chunk_rows, axis=0))
    vs = jnp.concatenate(group_cols, axis=1)
    gated = (u * vs).astype(BF16)
    o_ref[...] = x + jnp.dot(gated, wout_ref[...], preferred_element_type=F32)


def _sgu(x2, g, w_in, ln_g, ln_b, w_s, b_s, w_out):
    T = x2.shape[0]
    tm = 256
    row = pl.BlockSpec((tm, D_MODEL), lambda i: (i, 0))
    return pl.pallas_call(
        _sgu_kernel,
        grid=(T // tm,),
        in_specs=[row, _const_spec((1, D_MODEL)), _const_spec(w_in.shape),
                  _const_spec((1, D_A)), _const_spec((1, D_A)),
                  _const_spec(w_s.shape), _const_spec((CHUNK, SGU_GROUPS)),
                  _const_spec(w_out.shape)],
        out_specs=row,
        out_shape=jax.ShapeDtypeStruct((T, D_MODEL), F32),
        compiler_params=_params(1),
        name="sgu",
    )(x2, g.reshape(1, D_MODEL), w_in, ln_g.reshape(1, D_A), ln_b.reshape(1, D_A),
      w_s, b_s.T, w_out)


def _norm_proj_kernel(x_ref, g_ref, w_ref, *o_refs, scale):
    h = _rms_norm(x_ref[0], g_ref[...]).astype(BF16)
    y = jnp.dot(h, w_ref[...], preferred_element_type=F32)
    if scale != 1.0:
        y = y * scale
    y = y.astype(BF16)
    heads_per_out = o_refs[0].shape[1]
    for oi, o_ref in enumerate(o_refs):
        for hd in range(heads_per_out):
            c0 = (oi * heads_per_out + hd) * SB_HEAD_DIM
            o_ref[0, hd] = y[:, c0:c0 + SB_HEAD_DIM]


def _norm_proj(x3, g, w, n_out, scale):
    B, S, _ = x3.shape
    heads = w.shape[1] // (n_out * SB_HEAD_DIM)
    tm = TOKEN_TILE
    out_spec = pl.BlockSpec((1, heads, tm, SB_HEAD_DIM), lambda b, i: (b, 0, i, 0))
    out_shape = jax.ShapeDtypeStruct((B, heads, S, SB_HEAD_DIM), BF16)
    return pl.pallas_call(
        functools.partial(_norm_proj_kernel, scale=scale),
        grid=(B, S // tm),
        in_specs=[pl.BlockSpec((1, tm, D_MODEL), lambda b, i: (b, i, 0)),
                  _const_spec((1, D_MODEL)), _const_spec(w.shape)],
        out_specs=[out_spec] * n_out,
        out_shape=[out_shape] * n_out,
        compiler_params=_params(2),
        name="norm_proj",
    )(x3, g.reshape(1, D_MODEL), w)


def _attn_kernel(q_ref, k_ref, v_ref, o_ref, acc_ref, run_ref):
    qt = pl.program_id(2)
    blk = ATT_BLOCK
    nsub = ATT_Q_SUB
    row = lax.broadcasted_iota(jnp.int32, (blk, blk), 0)
    col = lax.broadcasted_iota(jnp.int32, (blk, blk), 1)
    suffix = jnp.where(row > col, -1.0, 0.0).astype(BF16)
    suffix2 = jnp.concatenate([suffix, suffix], axis=0)
    strictly_earlier = col < row
    sign_bit = jnp.uint32(0x80000000)

    def load_kv(j):
        rows = pl.ds(pl.multiple_of(j * blk, blk), blk)
        return k_ref[0, 0, rows, :], v_ref[0, 0, rows, :]

    def scores(unit):
        sub, hd, (kb, _), diag = unit
        idx = sub * SB_GROUP + hd
        z = lax.dot_general(q_ref[0, hd, sub * blk:(sub + 1) * blk, :], kb,
                            (((1,), (1,)), ((), ())), preferred_element_type=F32)
        neg_abs = lax.bitcast_convert_type(lax.bitcast_convert_type(z, jnp.uint32) | sign_bit, F32)
        sp = jnp.maximum(z, 0.0) + jnp.log2(1.0 + jnp.exp2(neg_abs))
        if diag:
            sp = jnp.where(strictly_earlier, sp, 0.0)
        hi = sp.astype(BF16)
        lo = (sp - hi.astype(F32)).astype(BF16)
        run = 0.0 if diag else run_ref[idx]
        run_ref[idx] = run - jnp.broadcast_to(jnp.sum(sp, axis=-1, keepdims=True), (blk, 128))
        w = z - sp
        if not diag:
            w = w + jnp.concatenate([run, run], axis=1)
        return jnp.concatenate([hi, lo], axis=1), w

    def weights(unit, st):
        hilo, w = st
        a = jnp.exp2(w + jnp.dot(hilo, suffix2, preferred_element_type=F32))
        if unit[3]:
            a = jnp.where(strictly_earlier, a, 0.0)
        return a.astype(BF16)

    def accumulate(unit, a):
        sub, hd, (_, vb), diag = unit
        idx = sub * SB_GROUP + hd
        pv = jnp.dot(a, vb, preferred_element_type=F32)
        if diag:
            acc_ref[idx] = pv
        else:
            acc_ref[idx] += pv

    def run_units(units):
        n = len(units)
        st, a = {}, {}
        for step in range(n + 2):
            if step < n:
                st[step] = scores(units[step])
            if 0 <= step - 1 < n:
                a[step - 1] = weights(units[step - 1], st.pop(step - 1))
            if step - 2 >= 0:
                accumulate(units[step - 2], a.pop(step - 2))

    units = []
    for b in reversed(range(nsub)):
        kv = load_kv(qt * nsub + b)
        units += [(sub, hd, kv, sub == b) for sub in range(b, nsub) for hd in range(SB_GROUP)]
    run_units(units)

    @pl.loop(0, qt * nsub)
    def _(i):
        kv = load_kv(qt * nsub - 1 - i)
        run_units([(sub, hd, kv, False) for sub in range(nsub) for hd in range(SB_GROUP)])

    for sub in range(nsub):
        o_ref[0, sub * blk:(sub + 1) * blk, :] = jnp.concatenate(
            [acc_ref[sub * SB_GROUP + hd] for hd in range(SB_GROUP)], axis=1).astype(o_ref.dtype)


def _attention(q, k, v):
    B, _, S, _ = q.shape
    tq = ATT_BLOCK * ATT_Q_SUB
    kv_spec = pl.BlockSpec((1, 1, S, SB_HEAD_DIM), lambda b, kh, i: (b, kh, 0, 0))
    return pl.pallas_call(
        _attn_kernel,
        grid=(B, SB_KV_HEADS, S // tq),
        in_specs=[pl.BlockSpec((1, SB_GROUP, tq, SB_HEAD_DIM), lambda b, kh, i: (b, kh, i, 0)),
                  kv_spec, kv_spec],
        out_specs=pl.BlockSpec((1, tq, SB_GROUP * SB_HEAD_DIM), lambda b, kh, i: (b, i, kh)),
        out_shape=jax.ShapeDtypeStruct((B, S, SB_HEADS * SB_HEAD_DIM), BF16),
        scratch_shapes=[pltpu.VMEM((ATT_Q_SUB * SB_GROUP, ATT_BLOCK, SB_HEAD_DIM), F32),
                        pltpu.VMEM((ATT_Q_SUB * SB_GROUP, ATT_BLOCK, 128), F32)],
        compiler_params=_params(3),
        name="sb_attention",
    )(q, k, v)


def _proj_res_kernel(o_ref, w_ref, x_ref, out_ref):
    out_ref[...] = x_ref[...] + jnp.dot(o_ref[...], w_ref[...], preferred_element_type=F32)


def _proj_res(o2, w, x2):
    T = x2.shape[0]
    row = pl.BlockSpec((TOKEN_TILE, D_MODEL), lambda i: (i, 0))
    return pl.pallas_call(
        _proj_res_kernel,
        grid=(T // TOKEN_TILE,),
        in_specs=[row, _const_spec(w.shape), row],
        out_specs=row,
        out_shape=jax.ShapeDtypeStruct((T, D_MODEL), F32),
        compiler_params=_params(1),
        name="proj_res",
    )(o2, w, x2)


def kernel(x, ffn_norm, ffn_w_in, ffn_w_out, mix_norm, a_w_in, a_ln_g, a_ln_b, a_w_s, a_b_s,
           a_w_out, kv_norm, w_kv, b_w_q, b_w_o, final_norm):
    B, S, D = x.shape
    depth = ffn_norm.shape[0]
    n_a = a_w_in.shape[0]
    w_in = ffn_w_in.astype(BF16)
    w_out = ffn_w_out.astype(BF16)
    x2 = x.reshape(B * S, D)
    k = v = None
    for i in range(depth):
        if i == n_a:
            k, v = _norm_proj(x2.reshape(B, S, D), kv_norm, w_kv.astype(BF16), 2, 1.0)
        x2 = _ffn(x2, ffn_norm[i, 0], w_in[i, 0], w_out[i, 0])
        if i < n_a:
            x2 = _sgu(x2, mix_norm[i], a_w_in[i].astype(BF16), a_ln_g[i], a_ln_b[i],
                      a_w_s[i], a_b_s[i], a_w_out[i].astype(BF16))
        else:
            j = i - n_a
            (q,) = _norm_proj(x2.reshape(B, S, D), mix_norm[i], b_w_q[j].astype(BF16), 1,
                              LOG2E / math.sqrt(SB_HEAD_DIM))
            o = _attention(q, k, v)
            x2 = _proj_res(o.reshape(B * S, D), b_w_o[j].astype(BF16), x2)
        x2 = _ffn(x2, ffn_norm[i, 1], w_in[i, 1], w_out[i, 1],
                  final_g=final_norm if i == depth - 1 else None)
    return x2.reshape(B, S, D)
```

```python
import functools
import math

import jax
import jax.numpy as jnp
from jax import lax
from jax.experimental import pallas as pl
from jax.experimental.pallas import tpu as pltpu

D_MODEL = 1024
D_FF = 2816
D_A = 2 * D_MODEL
CHUNK = 128
SGU_GROUPS = 8
SGU_GROUP_DIM = D_A // SGU_GROUPS
SB_HEADS = 16
SB_HEAD_DIM = 64
SB_KV_HEADS = 4
SB_GROUP = SB_HEADS // SB_KV_HEADS
KV_DIM = SB_KV_HEADS * SB_HEAD_DIM
FFN_RES_SCALE = 0.5
NORM_EPS = 1e-6
LN_EPS = 1e-5
LOG2E = 1.4426950408889634
SOFTPLUS_CLAMP = 64.0

VMEM_LIMIT_BYTES = 56 * 1024 * 1024

TOKEN_TILE = 512
FFN_CHUNK = 256
ATT_BLOCK = 256
ATT_Q_SUB = 4

BF16 = jnp.bfloat16
F32 = jnp.float32


def _const_spec(shape):
    nd = len(shape)
    return pl.BlockSpec(shape, lambda *_: (0,) * nd, pipeline_mode=pl.Buffered(1))


def _rms_norm(x, g):
    return x * lax.rsqrt(jnp.mean(x * x, axis=-1, keepdims=True) + NORM_EPS) * g


def _params(n_grid_dims):
    return pltpu.CompilerParams(
        dimension_semantics=("parallel",) * n_grid_dims,
        vmem_limit_bytes=VMEM_LIMIT_BYTES)


def _ffn_kernel(x_ref, g_ref, win_ref, wout_ref, *rest, final):
    if final:
        fg_ref, o_ref = rest
    else:
        (o_ref,) = rest
    x = x_ref[...]
    h = _rms_norm(x, g_ref[...]).astype(BF16)
    acc = jnp.zeros(x.shape, F32)
    for c in range(D_FF // FFN_CHUNK):
        lo = c * FFN_CHUNK
        gate = jnp.dot(h, win_ref[:, lo:lo + FFN_CHUNK], preferred_element_type=F32)
        up = jnp.dot(h, win_ref[:, D_FF + lo:D_FF + lo + FFN_CHUNK], preferred_element_type=F32)
        act = (gate * jax.nn.sigmoid(gate) * up).astype(BF16)
        acc = acc + jnp.dot(act, wout_ref[lo:lo + FFN_CHUNK, :], preferred_element_type=F32)
    y = x + FFN_RES_SCALE * acc
    if final:
        y = _rms_norm(y, fg_ref[...])
    o_ref[...] = y


def _ffn(x2, g, w_in, w_out, final_g=None):
    T = x2.shape[0]
    final = final_g is not None
    row = pl.BlockSpec((TOKEN_TILE, D_MODEL), lambda i: (i, 0))
    in_specs = [row, _const_spec((1, D_MODEL)), _const_spec(w_in.shape), _const_spec(w_out.shape)]
    args = [x2, g.reshape(1, D_MODEL), w_in, w_out]
    if final:
        in_specs.append(_const_spec((1, D_MODEL)))
        args.append(final_g.reshape(1, D_MODEL))
    return pl.pallas_call(
        functools.partial(_ffn_kernel, final=final),
        grid=(T // TOKEN_TILE,),
        in_specs=in_specs,
        out_specs=row,
        out_shape=jax.ShapeDtypeStruct((T, D_MODEL), F32),
        compiler_params=_params(1),
        name="ffn_final" if final else "ffn",
    )(*args)


def _sgu_kernel(x_ref, g_ref, win_ref, lng_ref, lnb_ref, ws_ref, bs_ref, wout_ref, o_ref):
    x = x_ref[...]
    tm = x.shape[0]
    h = _rms_norm(x, g_ref[...]).astype(BF16)
    u = jax.nn.gelu(jnp.dot(h, win_ref[:, :D_A], preferred_element_type=F32))
    v = jax.nn.gelu(jnp.dot(h, win_ref[:, D_A:], preferred_element_type=F32))
    mu = jnp.mean(v, axis=-1, keepdims=True)
    vc = v - mu
    var = jnp.mean(vc * vc, axis=-1, keepdims=True)
    vn = (vc * lax.rsqrt(var + LN_EPS) * lng_ref[...] + lnb_ref[...]).astype(BF16)

    row = lax.broadcasted_iota(jnp.int32, (CHUNK, CHUNK), 0)
    col = lax.broadcasted_iota(jnp.int32, (CHUNK, CHUNK), 1)
    causal = row >= col
    bs = bs_ref[...]
    group_cols = []
    for gi in range(SGU_GROUPS):
        ws = jnp.where(causal, ws_ref[gi], 0.0).astype(BF16)
        bias = bs[:, gi:gi + 1]
        c0 = gi * SGU_GROUP_DIM
        chunk_rows = []
        for n in range(tm // CHUNK):
            vb = vn[n * CHUNK:(n + 1) * CHUNK, c0:c0 + SGU_GROUP_DIM]
            chunk_rows.append(jnp.dot(ws, vb, preferred_element_type=F32) + bias)
        group_cols.append(jnp.concatenate(chunk_rows, axis=0))
    vs = jnp.concatenate(group_cols, axis=1)
    gated = (u * vs).astype(BF16)
    o_ref[...] = x + jnp.dot(gated, wout_ref[...], preferred_element_type=F32)


def _sgu(x2, g, w_in, ln_g, ln_b, w_s, b_s, w_out):
    T = x2.shape[0]
    tm = 256
    row = pl.BlockSpec((tm, D_MODEL), lambda i: (i, 0))
    return pl.pallas_call(
        _sgu_kernel,
        grid=(T // tm,),
        in_specs=[row, _const_spec((1, D_MODEL)), _const_spec(w_in.shape),
                  _const_spec((1, D_A)), _const_spec((1, D_A)),
                  _const_spec(w_s.shape), _const_spec((CHUNK, SGU_GROUPS)),
                  _const_spec(w_out.shape)],
        out_specs=row,
        out_shape=jax.ShapeDtypeStruct((T, D_MODEL), F32),
        compiler_params=_params(1),
        name="sgu",
    )(x2, g.reshape(1, D_MODEL), w_in, ln_g.reshape(1, D_A), ln_b.reshape(1, D_A),
      w_s, b_s.T, w_out)


def _norm_proj_kernel(x_ref, g_ref, w_ref, *o_refs, scale):
    h = _rms_norm(x_ref[0], g_ref[...]).astype(BF16)
    y = jnp.dot(h, w_ref[...], preferred_element_type=F32)
    if scale != 1.0:
        y = y * scale
    y = y.astype(BF16)
    heads_per_out = o_refs[0].shape[1]
    for oi, o_ref in enumerate(o_refs):
        for hd in range(heads_per_out):
            c0 = (oi * heads_per_out + hd) * SB_HEAD_DIM
            o_ref[0, hd] = y[:, c0:c0 + SB_HEAD_DIM]


def _norm_proj(x3, g, w, n_out, scale):
    B, S, _ = x3.shape
    heads = w.shape[1] // (n_out * SB_HEAD_DIM)
    tm = TOKEN_TILE
    out_spec = pl.BlockSpec((1, heads, tm, SB_HEAD_DIM), lambda b, i: (b, 0, i, 0))
    out_shape = jax.ShapeDtypeStruct((B, heads, S, SB_HEAD_DIM), BF16)
    return pl.pallas_call(
        functools.partial(_norm_proj_kernel, scale=scale),
        grid=(B, S // tm),
        in_specs=[pl.BlockSpec((1, tm, D_MODEL), lambda b, i: (b, i, 0)),
                  _const_spec((1, D_MODEL)), _const_spec(w.shape)],
        out_specs=[out_spec] * n_out,
        out_shape=[out_shape] * n_out,
        compiler_params=_params(2),
        name="norm_proj",
    )(x3, g.reshape(1, D_MODEL), w)


def _attn_kernel(q_ref, k_ref, v_ref, o_ref, acc_ref, run_ref):
    qt = pl.program_id(2)
    blk = ATT_BLOCK
    nsub = ATT_Q_SUB
    row = lax.broadcasted_iota(jnp.int32, (blk, blk), 0)
    col = lax.broadcasted_iota(jnp.int32, (blk, blk), 1)
    suffix = jnp.where(row > col, -1.0, 0.0).astype(BF16)
    strictly_earlier = col < row

    def load_kv(j):
        rows = pl.ds(pl.multiple_of(j * blk, blk), blk)
        return k_ref[0, 0, rows, :], v_ref[0, 0, rows, :]

    def scores(unit):
        sub, hd, (kb, _), diag = unit
        idx = sub * SB_GROUP + hd
        z = lax.dot_general(q_ref[0, hd, sub * blk:(sub + 1) * blk, :], kb,
                            (((1,), (1,)), ((), ())), preferred_element_type=F32)
        sp = jnp.maximum(z, jnp.log2(1.0 + jnp.exp2(jnp.minimum(z, SOFTPLUS_CLAMP))))
        if diag:
            sp = jnp.where(strictly_earlier, sp, 0.0)
        run = 0.0 if diag else run_ref[idx]
        run_ref[idx] = run - jnp.broadcast_to(jnp.sum(sp, axis=-1, keepdims=True), (blk, 128))
        w = z - sp
        if not diag:
            w = w + jnp.concatenate([run, run], axis=1)
        return sp.astype(BF16), w

    def weights(unit, st):
        sp, w = st
        a = jnp.exp2(w + jnp.dot(sp, suffix, preferred_element_type=F32))
        if unit[3]:
            a = jnp.where(strictly_earlier, a, 0.0)
        return a.astype(BF16)

    def accumulate(unit, a):
        sub, hd, (_, vb), diag = unit
        idx = sub * SB_GROUP + hd
        pv = jnp.dot(a, vb, preferred_element_type=F32)
        if diag:
            acc_ref[idx] = pv
        else:
            acc_ref[idx] += pv

    def run_units(units):
        n = len(units)
        st, a = {}, {}
        for step in range(n + 2):
            if step < n:
                st[step] = scores(units[step])
            if 0 <= step - 1 < n:
                a[step - 1] = weights(units[step - 1], st.pop(step - 1))
            if step - 2 >= 0:
                accumulate(units[step - 2], a.pop(step - 2))

    units = []
    for b in reversed(range(nsub)):
        kv = load_kv(qt * nsub + b)
        units += [(sub, hd, kv, sub == b) for sub in range(b, nsub) for hd in range(SB_GROUP)]
    run_units(units)

    @pl.loop(0, qt * nsub)
    def _(i):
        kv = load_kv(qt * nsub - 1 - i)
        run_units([(sub, hd, kv, False) for sub in range(nsub) for hd in range(SB_GROUP)])

    for sub in range(nsub):
        o_ref[0, sub * blk:(sub + 1) * blk, :] = jnp.concatenate(
            [acc_ref[sub * SB_GROUP + hd] for hd in range(SB_GROUP)], axis=1).astype(o_ref.dtype)


def _attention(q, k, v):
    B, _, S, _ = q.shape
    tq = ATT_BLOCK * ATT_Q_SUB
    kv_spec = pl.BlockSpec((1, 1, S, SB_HEAD_DIM), lambda b, kh, i: (b, kh, 0, 0))
    return pl.pallas_call(
        _attn_kernel,
        grid=(B, SB_KV_HEADS, S // tq),
        in_specs=[pl.BlockSpec((1, SB_GROUP, tq, SB_HEAD_DIM), lambda b, kh, i: (b, kh, i, 0)),
                  kv_spec, kv_spec],
        out_specs=pl.BlockSpec((1, tq, SB_GROUP * SB_HEAD_DIM), lambda b, kh, i: (b, i, kh)),
        out_shape=jax.ShapeDtypeStruct((B, S, SB_HEADS * SB_HEAD_DIM), BF16),
        scratch_shapes=[pltpu.VMEM((ATT_Q_SUB * SB_GROUP, ATT_BLOCK, SB_HEAD_DIM), F32),
                        pltpu.VMEM((ATT_Q_SUB * SB_GROUP, ATT_BLOCK, 128), F32)],
        compiler_params=_params(3),
        name="sb_attention",
    )(q, k, v)


def _proj_res_kernel(o_ref, w_ref, x_ref, out_ref):
    out_ref[...] = x_ref[...] + jnp.dot(o_ref[...], w_ref[...], preferred_element_type=F32)


def _proj_res(o2, w, x2):
    T = x2.shape[0]
    row = pl.BlockSpec((TOKEN_TILE, D_MODEL), lambda i: (i, 0))
    return pl.pallas_call(
        _proj_res_kernel,
        grid=(T // TOKEN_TILE,),
        in_specs=[row, _const_spec(w.shape), row],
        out_specs=row,
        out_shape=jax.ShapeDtypeStruct((T, D_MODEL), F32),
        compiler_params=_params(1),
        name="proj_res",
    )(o2, w, x2)


def kernel(x, ffn_norm, ffn_w_in, ffn_w_out, mix_norm, a_w_in, a_ln_g, a_ln_b, a_w_s, a_b_s,
           a_w_out, kv_norm, w_kv, b_w_q, b_w_o, final_norm):
    B, S, D = x.shape
    depth = ffn_norm.shape[0]
    n_a = a_w_in.shape[0]
    w_in = ffn_w_in.astype(BF16)
    w_out = ffn_w_out.astype(BF16)
    x2 = x.reshape(B * S, D)
    k = v = None
    for i in range(depth):
        if i == n_a:
            k, v = _norm_proj(x2.reshape(B, S, D), kv_norm, w_kv.astype(BF16), 2, 1.0)
        x2 = _ffn(x2, ffn_norm[i, 0], w_in[i, 0], w_out[i, 0])
        if i < n_a:
            x2 = _sgu(x2, mix_norm[i], a_w_in[i].astype(BF16), a_ln_g[i], a_ln_b[i],
                      a_w_s[i], a_b_s[i], a_w_out[i].astype(BF16))
        else:
            j = i - n_a
            (q,) = _norm_proj(x2.reshape(B, S, D), mix_norm[i], b_w_q[j].astype(BF16), 1,
                              LOG2E / math.sqrt(SB_HEAD_DIM))
            o = _attention(q, k, v)
            x2 = _proj_res(o.reshape(B * S, D), b_w_o[j].astype(BF16), x2)
        x2 = _ffn(x2, ffn_norm[i, 1], w_in[i, 1], w_out[i, 1],
                  final_g=final_norm if i == depth - 1 else None)
    return x2.reshape(B, S, D)
```

```python
import functools
import math

import jax
import jax.numpy as jnp
from jax import lax
from jax.experimental import pallas as pl
from jax.experimental.pallas import tpu as pltpu

D_MODEL = 1024
D_FF = 2816
D_A = 2 * D_MODEL
CHUNK = 128
SGU_GROUPS = 8
SGU_GROUP_DIM = D_A // SGU_GROUPS
SB_HEADS = 16
SB_HEAD_DIM = 64
SB_KV_HEADS = 4
SB_GROUP = SB_HEADS // SB_KV_HEADS
KV_DIM = SB_KV_HEADS * SB_HEAD_DIM
FFN_RES_SCALE = 0.5
NORM_EPS = 1e-6
LN_EPS = 1e-5
LOG2E = 1.4426950408889634
SOFTPLUS_CLAMP = 64.0
DEAD_RUN = -160.0

VMEM_LIMIT_BYTES = 56 * 1024 * 1024

TOKEN_TILE = 512
FFN_CHUNK = 256
ATT_BLOCK = 256
ATT_Q_SUB = 2

BF16 = jnp.bfloat16
F32 = jnp.float32


def _const_spec(shape):
    nd = len(shape)
    return pl.BlockSpec(shape, lambda *_: (0,) * nd, pipeline_mode=pl.Buffered(1))


def _rms_norm(x, g):
    return x * lax.rsqrt(jnp.mean(x * x, axis=-1, keepdims=True) + NORM_EPS) * g


def _params(n_grid_dims):
    return pltpu.CompilerParams(
        dimension_semantics=("parallel",) * n_grid_dims,
        vmem_limit_bytes=VMEM_LIMIT_BYTES)


def _ffn_kernel(x_ref, g_ref, win_ref, wout_ref, *rest, final):
    if final:
        fg_ref, o_ref = rest
    else:
        (o_ref,) = rest
    x = x_ref[...]
    h = _rms_norm(x, g_ref[...]).astype(BF16)
    acc = jnp.zeros(x.shape, F32)
    for c in range(D_FF // FFN_CHUNK):
        lo = c * FFN_CHUNK
        gate = jnp.dot(h, win_ref[:, lo:lo + FFN_CHUNK], preferred_element_type=F32)
        up = jnp.dot(h, win_ref[:, D_FF + lo:D_FF + lo + FFN_CHUNK], preferred_element_type=F32)
        act = (gate * jax.nn.sigmoid(gate) * up).astype(BF16)
        acc = acc + jnp.dot(act, wout_ref[lo:lo + FFN_CHUNK, :], preferred_element_type=F32)
    y = x + FFN_RES_SCALE * acc
    if final:
        y = _rms_norm(y, fg_ref[...])
    o_ref[...] = y


def _ffn(x2, g, w_in, w_out, final_g=None):
    T = x2.shape[0]
    final = final_g is not None
    row = pl.BlockSpec((TOKEN_TILE, D_MODEL), lambda i: (i, 0))
    in_specs = [row, _const_spec((1, D_MODEL)), _const_spec(w_in.shape), _const_spec(w_out.shape)]
    args = [x2, g.reshape(1, D_MODEL), w_in, w_out]
    if final:
        in_specs.append(_const_spec((1, D_MODEL)))
        args.append(final_g.reshape(1, D_MODEL))
    return pl.pallas_call(
        functools.partial(_ffn_kernel, final=final),
        grid=(T // TOKEN_TILE,),
        in_specs=in_specs,
        out_specs=row,
        out_shape=jax.ShapeDtypeStruct((T, D_MODEL), F32),
        compiler_params=_params(1),
        name="ffn_final" if final else "ffn",
    )(*args)


def _sgu_kernel(x_ref, g_ref, win_ref, lng_ref, lnb_ref, ws_ref, bs_ref, wout_ref, o_ref):
    x = x_ref[...]
    tm = x.shape[0]
    h = _rms_norm(x, g_ref[...]).astype(BF16)
    u = jax.nn.gelu(jnp.dot(h, win_ref[:, :D_A], preferred_element_type=F32))
    v = jax.nn.gelu(jnp.dot(h, win_ref[:, D_A:], preferred_element_type=F32))
    mu = jnp.mean(v, axis=-1, keepdims=True)
    vc = v - mu
    var = jnp.mean(vc * vc, axis=-1, keepdims=True)
    vn = (vc * lax.rsqrt(var + LN_EPS) * lng_ref[...] + lnb_ref[...]).astype(BF16)

    row = lax.broadcasted_iota(jnp.int32, (CHUNK, CHUNK), 0)
    col = lax.broadcasted_iota(jnp.int32, (CHUNK, CHUNK), 1)
    causal = row >= col
    bs = bs_ref[...]
    group_cols = []
    for gi in range(SGU_GROUPS):
        ws = jnp.where(causal, ws_ref[gi], 0.0).astype(BF16)
        bias = bs[:, gi:gi + 1]
        c0 = gi * SGU_GROUP_DIM
        chunk_rows = []
        for n in range(tm // CHUNK):
            vb = vn[n * CHUNK:(n + 1) * CHUNK, c0:c0 + SGU_GROUP_DIM]
            chunk_rows.append(jnp.dot(ws, vb, preferred_element_type=F32) + bias)
        group_cols.append(jnp.concatenate(chunk_rows, axis=0))
    vs = jnp.concatenate(group_cols, axis=1)
    gated = (u * vs).astype(BF16)
    o_ref[...] = x + jnp.dot(gated, wout_ref[...], preferred_element_type=F32)


def _sgu(x2, g, w_in, ln_g, ln_b, w_s, b_s, w_out):
    T = x2.shape[0]
    tm = 256
    row = pl.BlockSpec((tm, D_MODEL), lambda i: (i, 0))
    return pl.pallas_call(
        _sgu_kernel,
        grid=(T // tm,),
        in_specs=[row, _const_spec((1, D_MODEL)), _const_spec(w_in.shape),
                  _const_spec((1, D_A)), _const_spec((1, D_A)),
                  _const_spec(w_s.shape), _const_spec((CHUNK, SGU_GROUPS)),
                  _const_spec(w_out.shape)],
        out_specs=row,
        out_shape=jax.ShapeDtypeStruct((T, D_MODEL), F32),
        compiler_params=_params(1),
        name="sgu",
    )(x2, g.reshape(1, D_MODEL), w_in, ln_g.reshape(1, D_A), ln_b.reshape(1, D_A),
      w_s, b_s.T, w_out)


def _norm_proj_kernel(x_ref, g_ref, w_ref, *o_refs, scale):
    h = _rms_norm(x_ref[0], g_ref[...]).astype(BF16)
    y = jnp.dot(h, w_ref[...], preferred_element_type=F32)
    if scale != 1.0:
        y = y * scale
    y = y.astype(BF16)
    heads_per_out = o_refs[0].shape[1]
    for oi, o_ref in enumerate(o_refs):
        for hd in range(heads_per_out):
            c0 = (oi * heads_per_out + hd) * SB_HEAD_DIM
            o_ref[0, hd] = y[:, c0:c0 + SB_HEAD_DIM]


def _norm_proj(x3, g, w, n_out, scale):
    B, S, _ = x3.shape
    heads = w.shape[1] // (n_out * SB_HEAD_DIM)
    tm = TOKEN_TILE
    out_spec = pl.BlockSpec((1, heads, tm, SB_HEAD_DIM), lambda b, i: (b, 0, i, 0))
    out_shape = jax.ShapeDtypeStruct((B, heads, S, SB_HEAD_DIM), BF16)
    return pl.pallas_call(
        functools.partial(_norm_proj_kernel, scale=scale),
        grid=(B, S // tm),
        in_specs=[pl.BlockSpec((1, tm, D_MODEL), lambda b, i: (b, i, 0)),
                  _const_spec((1, D_MODEL)), _const_spec(w.shape)],
        out_specs=[out_spec] * n_out,
        out_shape=[out_shape] * n_out,
        compiler_params=_params(2),
        name="norm_proj",
    )(x3, g.reshape(1, D_MODEL), w)


def _attn_kernel(q_ref, k_ref, v_ref, o_ref, acc_ref, run_ref):
    qt = pl.program_id(2)
    blk = ATT_BLOCK
    nsub = ATT_Q_SUB
    row = lax.broadcasted_iota(jnp.int32, (blk, blk), 0)
    col = lax.broadcasted_iota(jnp.int32, (blk, blk), 1)
    suffix = jnp.where(row > col, -1.0, 0.0).astype(BF16)
    strictly_earlier = col < row

    def load_kv(j):
        rows = pl.ds(pl.multiple_of(j * blk, blk), blk)
        return k_ref[0, 0, rows, :], v_ref[0, 0, rows, :]

    def scores(unit):
        sub, hd, (kb, _), diag = unit
        idx = sub * SB_GROUP + hd
        z = lax.dot_general(q_ref[0, hd, sub * blk:(sub + 1) * blk, :], kb,
                            (((1,), (1,)), ((), ())), preferred_element_type=F32)
        sp = jnp.maximum(z, jnp.log2(1.0 + jnp.exp2(jnp.minimum(z, SOFTPLUS_CLAMP))))
        if diag:
            sp = jnp.where(strictly_earlier, sp, 0.0)
        run = 0.0 if diag else run_ref[idx]
        new_run = run - jnp.broadcast_to(jnp.sum(sp, axis=-1, keepdims=True), (blk, 128))
        run_ref[idx] = new_run
        w = z - sp
        if not diag:
            w = w + jnp.concatenate([run, run], axis=1)
        return sp.astype(BF16), w, new_run

    def weights(unit, st):
        sp, w = st
        a = jnp.exp2(w + jnp.dot(sp, suffix, preferred_element_type=F32))
        if unit[3]:
            a = jnp.where(strictly_earlier, a, 0.0)
        return a.astype(BF16)

    def accumulate(unit, a):
        sub, hd, (_, vb), diag = unit
        idx = sub * SB_GROUP + hd
        pv = jnp.dot(a, vb, preferred_element_type=F32)
        if diag:
            acc_ref[idx] = pv
        else:
            acc_ref[idx] += pv

    def run_units(units):
        n = len(units)
        st, a = {}, {}
        max_run = None
        for step in range(n + 2):
            if step < n:
                sp, w, new_run = scores(units[step])
                st[step] = (sp, w)
                max_run = new_run if max_run is None else jnp.maximum(max_run, new_run)
            if 0 <= step - 1 < n:
                a[step - 1] = weights(units[step - 1], st.pop(step - 1))
            if step - 2 >= 0:
                accumulate(units[step - 2], a.pop(step - 2))
        return max_run

    units = []
    for b in reversed(range(nsub)):
        kv = load_kv(qt * nsub + b)
        units += [(sub, hd, kv, sub == b) for sub in range(b, nsub) for hd in range(SB_GROUP)]
    run_units(units)

    n_earlier = qt * nsub

    def more_blocks(carry):
        i, alive = carry
        return jnp.logical_and(i < n_earlier, alive > 0)

    def one_block(carry):
        i, _ = carry
        kv = load_kv(n_earlier - 1 - i)
        max_run = run_units([(sub, hd, kv, False) for sub in range(nsub) for hd in range(SB_GROUP)])
        return i + 1, (jnp.max(max_run) > DEAD_RUN).astype(jnp.int32)

    lax.while_loop(more_blocks, one_block, (jnp.int32(0), jnp.int32(1)))

    for sub in range(nsub):
        o_ref[0, sub * blk:(sub + 1) * blk, :] = jnp.concatenate(
            [acc_ref[sub * SB_GROUP + hd] for hd in range(SB_GROUP)], axis=1).astype(o_ref.dtype)


def _attention(q, k, v):
    B, _, S, _ = q.shape
    tq = ATT_BLOCK * ATT_Q_SUB
    kv_spec = pl.BlockSpec((1, 1, S, SB_HEAD_DIM), lambda b, kh, i: (b, kh, 0, 0))
    return pl.pallas_call(
        _attn_kernel,
        grid=(B, SB_KV_HEADS, S // tq),
        in_specs=[pl.BlockSpec((1, SB_GROUP, tq, SB_HEAD_DIM), lambda b, kh, i: (b, kh, i, 0)),
                  kv_spec, kv_spec],
        out_specs=pl.BlockSpec((1, tq, SB_GROUP * SB_HEAD_DIM), lambda b, kh, i: (b, i, kh)),
        out_shape=jax.ShapeDtypeStruct((B, S, SB_HEADS * SB_HEAD_DIM), BF16),
        scratch_shapes=[pltpu.VMEM((ATT_Q_SUB * SB_GROUP, ATT_BLOCK, SB_HEAD_DIM), F32),
                        pltpu.VMEM((ATT_Q_SUB * SB_GROUP, ATT_BLOCK, 128), F32)],
        compiler_params=_params(3),
        name="sb_attention",
    )(q, k, v)


def _proj_res_kernel(o_ref, w_ref, x_ref, out_ref):
    out_ref[...] = x_ref[...] + jnp.dot(o_ref[...], w_ref[...], preferred_element_type=F32)


def _proj_res(o2, w, x2):
    T = x2.shape[0]
    row = pl.BlockSpec((TOKEN_TILE, D_MODEL), lambda i: (i, 0))
    return pl.pallas_call(
        _proj_res_kernel,
        grid=(T // TOKEN_TILE,),
        in_specs=[row, _const_spec(w.shape), row],
        out_specs=row,
        out_shape=jax.ShapeDtypeStruct((T, D_MODEL), F32),
        compiler_params=_params(1),
        name="proj_res",
    )(o2, w, x2)


def kernel(x, ffn_norm, ffn_w_in, ffn_w_out, mix_norm, a_w_in, a_ln_g, a_ln_b, a_w_s, a_b_s,
           a_w_out, kv_norm, w_kv, b_w_q, b_w_o, final_norm):
    B, S, D = x.shape
    depth = ffn_norm.shape[0]
    n_a = a_w_in.shape[0]
    w_in = ffn_w_in.astype(BF16)
    w_out = ffn_w_out.astype(BF16)
    x2 = x.reshape(B * S, D)
    k = v = None
    for i in range(depth):
        if i == n_a:
            k, v = _norm_proj(x2.reshape(B, S, D), kv_norm, w_kv.astype(BF16), 2, 1.0)
        x2 = _ffn(x2, ffn_norm[i, 0], w_in[i, 0], w_out[i, 0])
        if i < n_a:
            x2 = _sgu(x2, mix_norm[i], a_w_in[i].astype(BF16), a_ln_g[i], a_ln_b[i],
                      a_w_s[i], a_b_s[i], a_w_out[i].astype(BF16))
        else:
            j = i - n_a
            (q,) = _norm_proj(x2.reshape(B, S, D), mix_norm[i], b_w_q[j].astype(BF16), 1,
                              LOG2E / math.sqrt(SB_HEAD_DIM))
            o = _attention(q, k, v)
            x2 = _proj_res(o.reshape(B * S, D), b_w_o[j].astype(BF16), x2)
        x2 = _ffn(x2, ffn_norm[i, 1], w_in[i, 1], w_out[i, 1],
                  final_g=final_norm if i == depth - 1 else None)
    return x2.reshape(B, S, D)
```

```python
import functools
import math
from typing import NamedTuple

import jax
import jax.numpy as jnp
from jax import lax
from jax.experimental import pallas as pl
from jax.experimental.pallas import tpu as pltpu

D_MODEL = 1024
D_FF = 2816
D_A = 2 * D_MODEL
CHUNK = 128
SGU_GROUPS = 8
SGU_GROUP_DIM = D_A // SGU_GROUPS
SB_HEADS = 16
SB_HEAD_DIM = 64
SB_KV_HEADS = 4
SB_GROUP = SB_HEADS // SB_KV_HEADS
FFN_RES_SCALE = 0.5
NORM_EPS = 1e-6
LN_EPS = 1e-5
LOG2E = 1.4426950408889634
SOFTPLUS_CLAMP = 64.0
DEAD_RUN = -160.0

VMEM_LIMIT_BYTES = 56 * 1024 * 1024

TOKEN_TILE = 512
FFN_CHUNK = 256
ATT_BLOCK = 256
ATT_Q_SUB = 2

BF16 = jnp.bfloat16
F32 = jnp.float32


def _const_spec(shape):
    nd = len(shape)
    return pl.BlockSpec(shape, lambda *_: (0,) * nd, pipeline_mode=pl.Buffered(1))


def _rms_norm(x, g):
    return x * lax.rsqrt(jnp.mean(x * x, axis=-1, keepdims=True) + NORM_EPS) * g


def _gelu_tanh(x):
    c1 = math.sqrt(2.0 / math.pi)
    c2 = c1 * 0.044715
    return x * (0.5 + 0.5 * jnp.tanh(x * (c1 + c2 * (x * x))))


def _params(n_grid_dims):
    return pltpu.CompilerParams(
        dimension_semantics=("parallel",) * n_grid_dims,
        vmem_limit_bytes=VMEM_LIMIT_BYTES)


class _FfnPlan(NamedTuple):
    attn_in: bool
    final_norm: bool
    proj_outs: int
    proj_scale: float


def _ffn_kernel(*refs, plan):
    it = iter(refs)
    x_ref = next(it)
    if plan.attn_in:
        o_ref, wo_ref = next(it), next(it)
    g_ref, win_ref, wout_ref = next(it), next(it), next(it)
    if plan.final_norm:
        fg_ref = next(it)
    if plan.proj_outs:
        pg_ref, pw_ref = next(it), next(it)
    y_ref = next(it)
    proj_refs = list(it)

    x = x_ref[...]
    if plan.attn_in:
        x = x + jnp.dot(o_ref[...], wo_ref[...], preferred_element_type=F32)
    h = _rms_norm(x, g_ref[...]).astype(BF16)
    acc = jnp.zeros(x.shape, F32)
    for c in range(D_FF // FFN_CHUNK):
        lo = c * FFN_CHUNK
        gate = jnp.dot(h, win_ref[:, lo:lo + FFN_CHUNK], preferred_element_type=F32)
        up = jnp.dot(h, win_ref[:, D_FF + lo:D_FF + lo + FFN_CHUNK], preferred_element_type=F32)
        act = (gate * jax.nn.sigmoid(gate) * up).astype(BF16)
        acc = acc + jnp.dot(act, wout_ref[lo:lo + FFN_CHUNK, :], preferred_element_type=F32)
    y = x + FFN_RES_SCALE * acc
    if plan.proj_outs:
        p = jnp.dot(_rms_norm(y, pg_ref[...]).astype(BF16), pw_ref[...],
                    preferred_element_type=F32)
        if plan.proj_scale != 1.0:
            p = p * plan.proj_scale
        p = p.astype(BF16)
        heads = proj_refs[0].shape[1]
        for oi, p_ref in enumerate(proj_refs):
            for hd in range(heads):
                c0 = (oi * heads + hd) * SB_HEAD_DIM
                p_ref[0, hd] = p[:, c0:c0 + SB_HEAD_DIM]
    if plan.final_norm:
        y = _rms_norm(y, fg_ref[...])
    y_ref[...] = y


def _ffn(x2, seq_len, norms, w_in, w_out, layer, half, *, attn=None, final_g=None, proj=None):
    T = x2.shape[0]
    tm = TOKEN_TILE
    plan = _FfnPlan(attn is not None, final_g is not None,
                    0 if proj is None else proj[2], 1.0 if proj is None else proj[3])
    row = pl.BlockSpec((tm, D_MODEL), lambda i: (i, 0))

    def stacked(shape):
        return pl.BlockSpec((None, None) + shape, lambda i: (layer, half, 0, 0),
                            pipeline_mode=pl.Buffered(1))

    in_specs, args = [row], [x2]
    if plan.attn_in:
        o2, w_o = attn
        in_specs += [row, _const_spec(w_o.shape)]
        args += [o2, w_o]
    in_specs += [stacked((1, D_MODEL)), stacked(w_in.shape[2:]), stacked(w_out.shape[2:])]
    args += [norms.reshape(norms.shape[0], 2, 1, D_MODEL), w_in, w_out]
    if plan.final_norm:
        in_specs.append(_const_spec((1, D_MODEL)))
        args.append(final_g.reshape(1, D_MODEL))
    out_specs, out_shape = [row], [jax.ShapeDtypeStruct((T, D_MODEL), F32)]
    if plan.proj_outs:
        p_g, p_w = proj[0], proj[1]
        in_specs += [_const_spec((1, D_MODEL)), _const_spec(p_w.shape)]
        args += [p_g.reshape(1, D_MODEL), p_w]
        heads = p_w.shape[1] // (plan.proj_outs * SB_HEAD_DIM)
        tiles_per_seq = seq_len // tm
        out_specs += [pl.BlockSpec((1, heads, tm, SB_HEAD_DIM),
                                   lambda i: (i // tiles_per_seq, 0, i % tiles_per_seq, 0))
                      ] * plan.proj_outs
        out_shape += [jax.ShapeDtypeStruct((T // seq_len, heads, seq_len, SB_HEAD_DIM), BF16)
                      ] * plan.proj_outs
    return pl.pallas_call(
        functools.partial(_ffn_kernel, plan=plan),
        grid=(T // tm,),
        in_specs=in_specs,
        out_specs=out_specs,
        out_shape=out_shape,
        compiler_params=_params(1),
        name="ffn_l%d_h%d" % (layer, half),
    )(*args)


def _sgu_kernel(x_ref, g_ref, win_ref, lng_ref, lnb_ref, ws_ref, bs_ref, wout_ref, o_ref):
    x = x_ref[...]
    tm = x.shape[0]
    h = _rms_norm(x, g_ref[...]).astype(BF16)
    u = _gelu_tanh(jnp.dot(h, win_ref[:, :D_A], preferred_element_type=F32))
    v = _gelu_tanh(jnp.dot(h, win_ref[:, D_A:], preferred_element_type=F32))
    mu = jnp.mean(v, axis=-1, keepdims=True)
    vc = v - mu
    var = jnp.mean(vc * vc, axis=-1, keepdims=True)
    vn = (vc * lax.rsqrt(var + LN_EPS) * lng_ref[...] + lnb_ref[...]).astype(BF16)

    row = lax.broadcasted_iota(jnp.int32, (CHUNK, CHUNK), 0)
    col = lax.broadcasted_iota(jnp.int32, (CHUNK, CHUNK), 1)
    causal = row >= col
    bs = bs_ref[...]
    group_cols = []
    for gi in range(SGU_GROUPS):
        ws = jnp.where(causal, ws_ref[gi], 0.0).astype(BF16)
        bias = bs[:, gi:gi + 1]
        c0 = gi * SGU_GROUP_DIM
        chunk_rows = []
        for n in range(tm // CHUNK):
            vb = vn[n * CHUNK:(n + 1) * CHUNK, c0:c0 + SGU_GROUP_DIM]
            chunk_rows.append(jnp.dot(ws, vb, preferred_element_type=F32) + bias)
        group_cols.append(jnp.concatenate(chunk_rows, axis=0))
    vs = jnp.concatenate(group_cols, axis=1)
    gated = (u * vs).astype(BF16)
    o_ref[...] = x + jnp.dot(gated, wout_ref[...], preferred_element_type=F32)


def _sgu(x2, g, w_in, ln_g, ln_b, w_s, b_s, w_out):
    T = x2.shape[0]
    tm = TOKEN_TILE
    row = pl.BlockSpec((tm, D_MODEL), lambda i: (i, 0))
    return pl.pallas_call(
        _sgu_kernel,
        grid=(T // tm,),
        in_specs=[row, _const_spec((1, D_MODEL)), _const_spec(w_in.shape),
                  _const_spec((1, D_A)), _const_spec((1, D_A)),
                  _const_spec(w_s.shape), _const_spec((CHUNK, SGU_GROUPS)),
                  _const_spec(w_out.shape)],
        out_specs=row,
        out_shape=jax.ShapeDtypeStruct((T, D_MODEL), F32),
        compiler_params=_params(1),
        name="sgu",
    )(x2, g.reshape(1, D_MODEL), w_in, ln_g.reshape(1, D_A), ln_b.reshape(1, D_A),
      w_s, b_s.T, w_out)


def _attn_kernel(q_ref, k_ref, v_ref, o_ref, acc_ref, run_ref):
    qt = pl.program_id(2)
    blk = ATT_BLOCK
    nsub = ATT_Q_SUB
    row = lax.broadcasted_iota(jnp.int32, (blk, blk), 0)
    col = lax.broadcasted_iota(jnp.int32, (blk, blk), 1)
    suffix = jnp.where(row > col, -1.0, 0.0).astype(BF16)
    strictly_earlier = col < row

    def load_kv(j):
        rows = pl.ds(pl.multiple_of(j * blk, blk), blk)
        return k_ref[0, 0, rows, :], v_ref[0, 0, rows, :]

    def scores(unit):
        sub, hd, (kb, _), diag = unit
        idx = sub * SB_GROUP + hd
        z = lax.dot_general(q_ref[0, hd, sub * blk:(sub + 1) * blk, :], kb,
                            (((1,), (1,)), ((), ())), preferred_element_type=F32)
        sp = jnp.maximum(z, jnp.log2(1.0 + jnp.exp2(jnp.minimum(z, SOFTPLUS_CLAMP))))
        if diag:
            sp = jnp.where(strictly_earlier, sp, 0.0)
        run = 0.0 if diag else run_ref[idx]
        new_run = run - jnp.broadcast_to(jnp.sum(sp, axis=-1, keepdims=True), (blk, 128))
        run_ref[idx] = new_run
        w = z - sp
        if not diag:
            w = w + jnp.concatenate([run, run], axis=1)
        return sp.astype(BF16), w, new_run

    def weights(unit, st):
        sp, w = st
        a = jnp.exp2(w + jnp.dot(sp, suffix, preferred_element_type=F32))
        if unit[3]:
            a = jnp.where(strictly_earlier, a, 0.0)
        return a.astype(BF16)

    def accumulate(unit, a):
        sub, hd, (_, vb), diag = unit
        idx = sub * SB_GROUP + hd
        pv = jnp.dot(a, vb, preferred_element_type=F32)
        if diag:
            acc_ref[idx] = pv
        else:
            acc_ref[idx] += pv

    def run_units(units):
        n = len(units)
        st, a = {}, {}
        max_run = None
        for step in range(n + 2):
            if step < n:
                sp, w, new_run = scores(units[step])
                st[step] = (sp, w)
                max_run = new_run if max_run is None else jnp.maximum(max_run, new_run)
            if 0 <= step - 1 < n:
                a[step - 1] = weights(units[step - 1], st.pop(step - 1))
            if step - 2 >= 0:
                accumulate(units[step - 2], a.pop(step - 2))
        return max_run

    units = []
    for b in reversed(range(nsub)):
        kv = load_kv(qt * nsub + b)
        units += [(sub, hd, kv, sub == b) for sub in range(b, nsub) for hd in range(SB_GROUP)]
    run_units(units)

    n_earlier = qt * nsub

    def more_blocks(carry):
        i, alive = carry
        return jnp.logical_and(i < n_earlier, alive > 0)

    def one_block(carry):
        i, _ = carry
        kv = load_kv(n_earlier - 1 - i)
        max_run = run_units([(sub, hd, kv, False) for sub in range(nsub) for hd in range(SB_GROUP)])
        return i + 1, (jnp.max(max_run) > DEAD_RUN).astype(jnp.int32)

    lax.while_loop(more_blocks, one_block, (jnp.int32(0), jnp.int32(1)))

    for sub in range(nsub):
        o_ref[0, sub * blk:(sub + 1) * blk, :] = jnp.concatenate(
            [acc_ref[sub * SB_GROUP + hd] for hd in range(SB_GROUP)], axis=1).astype(o_ref.dtype)


def _attention(q, k, v):
    B, _, S, _ = q.shape
    tq = ATT_BLOCK * ATT_Q_SUB
    kv_spec = pl.BlockSpec((1, 1, S, SB_HEAD_DIM), lambda b, kh, i: (b, kh, 0, 0))
    return pl.pallas_call(
        _attn_kernel,
        grid=(B, SB_KV_HEADS, S // tq),
        in_specs=[pl.BlockSpec((1, SB_GROUP, tq, SB_HEAD_DIM), lambda b, kh, i: (b, kh, i, 0)),
                  kv_spec, kv_spec],
        out_specs=pl.BlockSpec((1, tq, SB_GROUP * SB_HEAD_DIM), lambda b, kh, i: (b, i, kh)),
        out_shape=jax.ShapeDtypeStruct((B, S, SB_HEADS * SB_HEAD_DIM), BF16),
        scratch_shapes=[pltpu.VMEM((ATT_Q_SUB * SB_GROUP, ATT_BLOCK, SB_HEAD_DIM), F32),
                        pltpu.VMEM((ATT_Q_SUB * SB_GROUP, ATT_BLOCK, 128), F32)],
        compiler_params=_params(3),
        name="sb_attention",
    )(q, k, v)


def kernel(x, ffn_norm, ffn_w_in, ffn_w_out, mix_norm, a_w_in, a_ln_g, a_ln_b, a_w_s, a_b_s,
           a_w_out, kv_norm, w_kv, b_w_q, b_w_o, final_norm):
    B, S, D = x.shape
    depth = ffn_norm.shape[0]
    n_a = a_w_in.shape[0]
    assert 1 <= n_a < depth, "shared K/V are produced by the last mixer-A layer"
    w_in = ffn_w_in.astype(BF16)
    w_out = ffn_w_out.astype(BF16)
    ffn = functools.partial(_ffn, seq_len=S, norms=ffn_norm, w_in=w_in, w_out=w_out)
    q_scale = LOG2E / math.sqrt(SB_HEAD_DIM)
    x2 = x.reshape(B * S, D)
    k = v = None
    for i in range(depth):
        if i < n_a:
            (x2,) = ffn(x2, layer=i, half=0)
            x2 = _sgu(x2, mix_norm[i], a_w_in[i].astype(BF16), a_ln_g[i], a_ln_b[i],
                      a_w_s[i], a_b_s[i], a_w_out[i].astype(BF16))
            if i == n_a - 1:
                x2, k, v = ffn(x2, layer=i, half=1, proj=(kv_norm, w_kv.astype(BF16), 2, 1.0))
            else:
                (x2,) = ffn(x2, layer=i, half=1)
        else:
            j = i - n_a
            x2, q = ffn(x2, layer=i, half=0,
                        proj=(mix_norm[i], b_w_q[j].astype(BF16), 1, q_scale))
            o = _attention(q, k, v)
            (x2,) = ffn(x2, layer=i, half=1, attn=(o.reshape(B * S, D), b_w_o[j].astype(BF16)),
                        final_g=final_norm if i == depth - 1 else None)
    return x2.reshape(B, S, D)
```

```python
import functools
import math
from typing import NamedTuple

import jax
import jax.numpy as jnp
from jax import lax
from jax.experimental import pallas as pl
from jax.experimental.pallas import tpu as pltpu

D_MODEL = 1024
D_FF = 2816
D_A = 2 * D_MODEL
CHUNK = 128
SGU_GROUPS = 8
SGU_GROUP_DIM = D_A // SGU_GROUPS
SB_HEADS = 16
SB_HEAD_DIM = 64
SB_KV_HEADS = 4
SB_GROUP = SB_HEADS // SB_KV_HEADS
FFN_RES_SCALE = 0.5
NORM_EPS = 1e-6
LN_EPS = 1e-5
LOG2E = 1.4426950408889634
SOFTPLUS_CLAMP = 64.0
DEAD_RUN = -160.0
FINISHED_RUN = -1e30

VMEM_LIMIT_BYTES = 56 * 1024 * 1024

TOKEN_TILE = 512
FFN_CHUNK = 256
ATT_BLOCK = 256
ATT_Q_SUB = 4

BF16 = jnp.bfloat16
F32 = jnp.float32


def _const_spec(shape):
    nd = len(shape)
    return pl.BlockSpec(shape, lambda *_: (0,) * nd, pipeline_mode=pl.Buffered(1))


def _rms_norm(x, g):
    return x * lax.rsqrt(jnp.mean(x * x, axis=-1, keepdims=True) + NORM_EPS) * g


def _gelu_tanh(x):
    c1 = math.sqrt(2.0 / math.pi)
    c2 = c1 * 0.044715
    return x * (0.5 + 0.5 * jnp.tanh(x * (c1 + c2 * (x * x))))


def _params(n_grid_dims):
    return pltpu.CompilerParams(
        dimension_semantics=("parallel",) * n_grid_dims,
        vmem_limit_bytes=VMEM_LIMIT_BYTES)


class _FfnPlan(NamedTuple):
    attn_in: bool
    final_norm: bool
    proj_outs: int
    proj_scale: float


def _ffn_kernel(*refs, plan):
    it = iter(refs)
    x_ref = next(it)
    if plan.attn_in:
        o_ref, wo_ref = next(it), next(it)
    g_ref, win_ref, wout_ref = next(it), next(it), next(it)
    if plan.final_norm:
        fg_ref = next(it)
    if plan.proj_outs:
        pg_ref, pw_ref = next(it), next(it)
    y_ref = next(it)
    proj_refs = list(it)

    x = x_ref[...]
    if plan.attn_in:
        x = x + jnp.dot(o_ref[...], wo_ref[...], preferred_element_type=F32)
    h = _rms_norm(x, g_ref[...]).astype(BF16)
    acc = jnp.zeros(x.shape, F32)
    for c in range(D_FF // FFN_CHUNK):
        lo = c * FFN_CHUNK
        gate = jnp.dot(h, win_ref[:, lo:lo + FFN_CHUNK], preferred_element_type=F32)
        up = jnp.dot(h, win_ref[:, D_FF + lo:D_FF + lo + FFN_CHUNK], preferred_element_type=F32)
        act = (gate * jax.nn.sigmoid(gate) * up).astype(BF16)
        acc = acc + jnp.dot(act, wout_ref[lo:lo + FFN_CHUNK, :], preferred_element_type=F32)
    y = x + FFN_RES_SCALE * acc
    if plan.proj_outs:
        p = jnp.dot(_rms_norm(y, pg_ref[...]).astype(BF16), pw_ref[...],
                    preferred_element_type=F32)
        if plan.proj_scale != 1.0:
            p = p * plan.proj_scale
        p = p.astype(BF16)
        heads = proj_refs[0].shape[1]
        for oi, p_ref in enumerate(proj_refs):
            for hd in range(heads):
                c0 = (oi * heads + hd) * SB_HEAD_DIM
                p_ref[0, hd] = p[:, c0:c0 + SB_HEAD_DIM]
    if plan.final_norm:
        y = _rms_norm(y, fg_ref[...])
    y_ref[...] = y


def _ffn(x2, seq_len, norms, w_in, w_out, layer, half, *, attn=None, final_g=None, proj=None):
    T = x2.shape[0]
    tm = TOKEN_TILE
    plan = _FfnPlan(attn is not None, final_g is not None,
                    0 if proj is None else proj[2], 1.0 if proj is None else proj[3])
    row = pl.BlockSpec((tm, D_MODEL), lambda i: (i, 0))

    def stacked(shape):
        return pl.BlockSpec((None, None) + shape, lambda i: (layer, half, 0, 0),
                            pipeline_mode=pl.Buffered(1))

    in_specs, args = [row], [x2]
    if plan.attn_in:
        o2, w_o = attn
        in_specs += [row, _const_spec(w_o.shape)]
        args += [o2, w_o]
    in_specs += [stacked((1, D_MODEL)), stacked(w_in.shape[2:]), stacked(w_out.shape[2:])]
    args += [norms.reshape(norms.shape[0], 2, 1, D_MODEL), w_in, w_out]
    if plan.final_norm:
        in_specs.append(_const_spec((1, D_MODEL)))
        args.append(final_g.reshape(1, D_MODEL))
    out_specs, out_shape = [row], [jax.ShapeDtypeStruct((T, D_MODEL), F32)]
    if plan.proj_outs:
        p_g, p_w = proj[0], proj[1]
        in_specs += [_const_spec((1, D_MODEL)), _const_spec(p_w.shape)]
        args += [p_g.reshape(1, D_MODEL), p_w]
        heads = p_w.shape[1] // (plan.proj_outs * SB_HEAD_DIM)
        tiles_per_seq = seq_len // tm
        out_specs += [pl.BlockSpec((1, heads, tm, SB_HEAD_DIM),
                                   lambda i: (i // tiles_per_seq, 0, i % tiles_per_seq, 0))
                      ] * plan.proj_outs
        out_shape += [jax.ShapeDtypeStruct((T // seq_len, heads, seq_len, SB_HEAD_DIM), BF16)
                      ] * plan.proj_outs
    return pl.pallas_call(
        functools.partial(_ffn_kernel, plan=plan),
        grid=(T // tm,),
        in_specs=in_specs,
        out_specs=out_specs,
        out_shape=out_shape,
        compiler_params=_params(1),
        name="ffn_l%d_h%d" % (layer, half),
    )(*args)


def _sgu_kernel(x_ref, g_ref, win_ref, lng_ref, lnb_ref, ws_ref, bs_ref, wout_ref, o_ref):
    x = x_ref[...]
    tm = x.shape[0]
    h = _rms_norm(x, g_ref[...]).astype(BF16)
    u = _gelu_tanh(jnp.dot(h, win_ref[:, :D_A], preferred_element_type=F32))
    v = _gelu_tanh(jnp.dot(h, win_ref[:, D_A:], preferred_element_type=F32))
    mu = jnp.mean(v, axis=-1, keepdims=True)
    vc = v - mu
    var = jnp.mean(vc * vc, axis=-1, keepdims=True)
    vn = (vc * lax.rsqrt(var + LN_EPS) * lng_ref[...] + lnb_ref[...]).astype(BF16)

    row = lax.broadcasted_iota(jnp.int32, (CHUNK, CHUNK), 0)
    col = lax.broadcasted_iota(jnp.int32, (CHUNK, CHUNK), 1)
    causal = row >= col
    bs = bs_ref[...]
    group_cols = []
    for gi in range(SGU_GROUPS):
        ws = jnp.where(causal, ws_ref[gi], 0.0).astype(BF16)
        bias = bs[:, gi:gi + 1]
        c0 = gi * SGU_GROUP_DIM
        chunk_rows = []
        for n in range(tm // CHUNK):
            vb = vn[n * CHUNK:(n + 1) * CHUNK, c0:c0 + SGU_GROUP_DIM]
            chunk_rows.append(jnp.dot(ws, vb, preferred_element_type=F32) + bias)
        group_cols.append(jnp.concatenate(chunk_rows, axis=0))
    vs = jnp.concatenate(group_cols, axis=1)
    gated = (u * vs).astype(BF16)
    o_ref[...] = x + jnp.dot(gated, wout_ref[...], preferred_element_type=F32)


def _sgu(x2, g, w_in, ln_g, ln_b, w_s, b_s, w_out):
    T = x2.shape[0]
    tm = TOKEN_TILE
    row = pl.BlockSpec((tm, D_MODEL), lambda i: (i, 0))
    return pl.pallas_call(
        _sgu_kernel,
        grid=(T // tm,),
        in_specs=[row, _const_spec((1, D_MODEL)), _const_spec(w_in.shape),
                  _const_spec((1, D_A)), _const_spec((1, D_A)),
                  _const_spec(w_s.shape), _const_spec((CHUNK, SGU_GROUPS)),
                  _const_spec(w_out.shape)],
        out_specs=row,
        out_shape=jax.ShapeDtypeStruct((T, D_MODEL), F32),
        compiler_params=_params(1),
        name="sgu",
    )(x2, g.reshape(1, D_MODEL), w_in, ln_g.reshape(1, D_A), ln_b.reshape(1, D_A),
      w_s, b_s.T, w_out)


def _attn_kernel(q_ref, k_ref, v_ref, o_ref, acc_ref, run_ref):
    qt = pl.program_id(2)
    blk = ATT_BLOCK
    nsub = ATT_Q_SUB
    row = lax.broadcasted_iota(jnp.int32, (blk, blk), 0)
    col = lax.broadcasted_iota(jnp.int32, (blk, blk), 1)
    suffix = jnp.where(row > col, -1.0, 0.0).astype(BF16)
    strictly_earlier = col < row

    def load_kv(j):
        rows = pl.ds(pl.multiple_of(jnp.maximum(j, 0) * blk, blk), blk)
        return k_ref[0, 0, rows, :], v_ref[0, 0, rows, :]

    def finished_after(j):
        return jnp.where(j >= 1, 0.0, FINISHED_RUN).astype(F32)

    def scores(unit):
        sub, hd, (kb, _), diag, retire = unit
        idx = sub * SB_GROUP + hd
        z = lax.dot_general(q_ref[0, hd, sub * blk:(sub + 1) * blk, :], kb,
                            (((1,), (1,)), ((), ())), preferred_element_type=F32)
        sp = jnp.maximum(z, jnp.log2(1.0 + jnp.exp2(jnp.minimum(z, SOFTPLUS_CLAMP))))
        if diag:
            sp = jnp.where(strictly_earlier, sp, 0.0)
        run = 0.0 if diag else run_ref[idx]
        new_run = run - jnp.broadcast_to(jnp.sum(sp, axis=-1, keepdims=True), (blk, 128))
        if retire is not None:
            new_run = new_run + retire
        run_ref[idx] = new_run
        w = z - sp
        if not diag:
            w = w + jnp.concatenate([run, run], axis=1)
        return sp.astype(BF16), w, new_run

    def weights(unit, st):
        sp, w = st
        a = jnp.exp2(w + jnp.dot(sp, suffix, preferred_element_type=F32))
        if unit[3]:
            a = jnp.where(strictly_earlier, a, 0.0)
        return a.astype(BF16)

    def accumulate(unit, a):
        sub, hd, (_, vb), diag, _ = unit
        idx = sub * SB_GROUP + hd
        pv = jnp.dot(a, vb, preferred_element_type=F32)
        if diag:
            acc_ref[idx] = pv
        else:
            acc_ref[idx] += pv

    def run_units(units, tracked):
        n = len(units)
        st, a = {}, {}
        max_run = None
        for step in range(n + 2):
            if step < n:
                sp, w, new_run = scores(units[step])
                st[step] = (sp, w)
                if step in tracked:
                    max_run = new_run if max_run is None else jnp.maximum(max_run, new_run)
            if 0 <= step - 1 < n:
                a[step - 1] = weights(units[step - 1], st.pop(step - 1))
            if step - 2 >= 0:
                accumulate(units[step - 2], a.pop(step - 2))
        return max_run

    def alive(max_run):
        return (jnp.max(max_run) > DEAD_RUN).astype(jnp.int32)

    first = qt * nsub
    units, tracked = [], set()
    for off in range(nsub - 1, -2, -1):
        j = first + off
        kv = load_kv(j)
        retire = finished_after(j) if off <= 0 else None
        for sub in (off, off + 1):
            if 0 <= sub < nsub:
                for hd in range(SB_GROUP):
                    if sub == off + 1:
                        tracked.add(len(units))
                    units.append((sub, hd, kv, sub == off, retire))
    head_max = run_units(units, tracked)

    n_rounds = first + nsub - 2

    def more_rounds(carry):
        i, live = carry
        return jnp.logical_and(i < n_rounds, live > 0)

    def one_round(carry):
        i, _ = carry
        units = []
        for sub in range(nsub):
            j = first + sub - 2 - i
            kv = load_kv(j)
            retire = finished_after(j)
            units += [(sub, hd, kv, False, retire) for hd in range(SB_GROUP)]
        return i + 1, alive(run_units(units, set(range(len(units)))))

    lax.while_loop(more_rounds, one_round, (jnp.int32(0), alive(head_max)))

    for sub in range(nsub):
        o_ref[0, sub * blk:(sub + 1) * blk, :] = jnp.concatenate(
            [acc_ref[sub * SB_GROUP + hd] for hd in range(SB_GROUP)], axis=1).astype(o_ref.dtype)


def _attention(q, k, v):
    B, _, S, _ = q.shape
    tq = ATT_BLOCK * ATT_Q_SUB
    kv_spec = pl.BlockSpec((1, 1, S, SB_HEAD_DIM), lambda b, kh, i: (b, kh, 0, 0))
    return pl.pallas_call(
        _attn_kernel,
        grid=(B, SB_KV_HEADS, S // tq),
        in_specs=[pl.BlockSpec((1, SB_GROUP, tq, SB_HEAD_DIM), lambda b, kh, i: (b, kh, i, 0)),
                  kv_spec, kv_spec],
        out_specs=pl.BlockSpec((1, tq, SB_GROUP * SB_HEAD_DIM), lambda b, kh, i: (b, i, kh)),
        out_shape=jax.ShapeDtypeStruct((B, S, SB_HEADS * SB_HEAD_DIM), BF16),
        scratch_shapes=[pltpu.VMEM((ATT_Q_SUB * SB_GROUP, ATT_BLOCK, SB_HEAD_DIM), F32),
                        pltpu.VMEM((ATT_Q_SUB * SB_GROUP, ATT_BLOCK, 128), F32)],
        compiler_params=_params(3),
        name="sb_attention",
    )(q, k, v)


def kernel(x, ffn_norm, ffn_w_in, ffn_w_out, mix_norm, a_w_in, a_ln_g, a_ln_b, a_w_s, a_b_s,
           a_w_out, kv_norm, w_kv, b_w_q, b_w_o, final_norm):
    B, S, D = x.shape
    depth = ffn_norm.shape[0]
    n_a = a_w_in.shape[0]
    assert 1 <= n_a < depth, "shared K/V are produced by the last mixer-A layer"
    w_in = ffn_w_in.astype(BF16)
    w_out = ffn_w_out.astype(BF16)
    ffn = functools.partial(_ffn, seq_len=S, norms=ffn_norm, w_in=w_in, w_out=w_out)
    q_scale = LOG2E / math.sqrt(SB_HEAD_DIM)
    x2 = x.reshape(B * S, D)
    k = v = None
    for i in range(depth):
        if i < n_a:
            (x2,) = ffn(x2, layer=i, half=0)
            x2 = _sgu(x2, mix_norm[i], a_w_in[i].astype(BF16), a_ln_g[i], a_ln_b[i],
                      a_w_s[i], a_b_s[i], a_w_out[i].astype(BF16))
            if i == n_a - 1:
                x2, k, v = ffn(x2, layer=i, half=1, proj=(kv_norm, w_kv.astype(BF16), 2, 1.0))
            else:
                (x2,) = ffn(x2, layer=i, half=1)
        else:
            j = i - n_a
            x2, q = ffn(x2, layer=i, half=0,
                        proj=(mix_norm[i], b_w_q[j].astype(BF16), 1, q_scale))
            o = _attention(q, k, v)
            (x2,) = ffn(x2, layer=i, half=1, attn=(o.reshape(B * S, D), b_w_o[j].astype(BF16)),
                        final_g=final_norm if i == depth - 1 else None)
    return x2.reshape(B, S, D)
```

```python
import functools
import math
from typing import NamedTuple

import jax
import jax.numpy as jnp
from jax import lax
from jax.experimental import pallas as pl
from jax.experimental.pallas import tpu as pltpu

D_MODEL = 1024
D_FF = 2816
D_A = 2 * D_MODEL
CHUNK = 128
SGU_GROUPS = 8
SGU_GROUP_DIM = D_A // SGU_GROUPS
SB_HEADS = 16
SB_HEAD_DIM = 64
SB_KV_HEADS = 4
SB_GROUP = SB_HEADS // SB_KV_HEADS
FFN_RES_SCALE = 0.5
NORM_EPS = 1e-6
LN_EPS = 1e-5
LOG2E = 1.4426950408889634
SOFTPLUS_CLAMP = 64.0
DEAD_RUN = -160.0
FINISHED_RUN = -1e30

VMEM_LIMIT_BYTES = 56 * 1024 * 1024

TOKEN_TILE = 512
FFN_TILE = 1024
FFN_CHUNK = 256
ATT_BLOCK = 256
ATT_Q_SUB = 4

BF16 = jnp.bfloat16
F32 = jnp.float32


def _const_spec(shape):
    nd = len(shape)
    return pl.BlockSpec(shape, lambda *_: (0,) * nd, pipeline_mode=pl.Buffered(1))


def _rms_norm(x, g):
    return x * lax.rsqrt(jnp.mean(x * x, axis=-1, keepdims=True) + NORM_EPS) * g


def _gelu_tanh(x):
    c1 = math.sqrt(2.0 / math.pi)
    c2 = c1 * 0.044715
    return x * (0.5 + 0.5 * jnp.tanh(x * (c1 + c2 * (x * x))))


def _params(n_grid_dims):
    return pltpu.CompilerParams(
        dimension_semantics=("parallel",) * n_grid_dims,
        vmem_limit_bytes=VMEM_LIMIT_BYTES)


class _FfnPlan(NamedTuple):
    attn_in: bool
    final_norm: bool
    proj_outs: int
    proj_scale: float


def _ffn_kernel(*refs, plan):
    it = iter(refs)
    x_ref = next(it)
    if plan.attn_in:
        o_ref, wo_ref = next(it), next(it)
    g_ref, win_ref, wout_ref = next(it), next(it), next(it)
    if plan.final_norm:
        fg_ref = next(it)
    if plan.proj_outs:
        pg_ref, pw_ref = next(it), next(it)
    y_ref = next(it)
    proj_refs = list(it)

    x = x_ref[...]
    if plan.attn_in:
        x = x + jnp.dot(o_ref[...], wo_ref[...], preferred_element_type=F32)
    h = _rms_norm(x, g_ref[...]).astype(BF16)
    acc = jnp.zeros(x.shape, F32)
    for c in range(D_FF // FFN_CHUNK):
        lo = c * FFN_CHUNK
        gate = jnp.dot(h, win_ref[:, lo:lo + FFN_CHUNK], preferred_element_type=F32)
        up = jnp.dot(h, win_ref[:, D_FF + lo:D_FF + lo + FFN_CHUNK], preferred_element_type=F32)
        act = (gate * jax.nn.sigmoid(gate) * up).astype(BF16)
        acc = acc + jnp.dot(act, wout_ref[lo:lo + FFN_CHUNK, :], preferred_element_type=F32)
    y = x + FFN_RES_SCALE * acc
    if plan.proj_outs:
        p = jnp.dot(_rms_norm(y, pg_ref[...]).astype(BF16), pw_ref[...],
                    preferred_element_type=F32)
        if plan.proj_scale != 1.0:
            p = p * plan.proj_scale
        p = p.astype(BF16)
        heads = proj_refs[0].shape[1]
        for oi, p_ref in enumerate(proj_refs):
            for hd in range(heads):
                c0 = (oi * heads + hd) * SB_HEAD_DIM
                p_ref[0, hd] = p[:, c0:c0 + SB_HEAD_DIM]
    if plan.final_norm:
        y = _rms_norm(y, fg_ref[...])
    y_ref[...] = y


def _ffn(x2, seq_len, norms, w_in, w_out, layer, half, *, attn=None, final_g=None, proj=None):
    T = x2.shape[0]
    tm = FFN_TILE
    plan = _FfnPlan(attn is not None, final_g is not None,
                    0 if proj is None else proj[2], 1.0 if proj is None else proj[3])
    row = pl.BlockSpec((tm, D_MODEL), lambda i: (i, 0))

    def stacked(shape):
        return pl.BlockSpec((None, None) + shape, lambda i: (layer, half, 0, 0),
                            pipeline_mode=pl.Buffered(1))

    in_specs, args = [row], [x2]
    if plan.attn_in:
        o2, w_o = attn
        in_specs += [row, _const_spec(w_o.shape)]
        args += [o2, w_o]
    in_specs += [stacked((1, D_MODEL)), stacked(w_in.shape[2:]), stacked(w_out.shape[2:])]
    args += [norms.reshape(norms.shape[0], 2, 1, D_MODEL), w_in, w_out]
    if plan.final_norm:
        in_specs.append(_const_spec((1, D_MODEL)))
        args.append(final_g.reshape(1, D_MODEL))
    out_specs, out_shape = [row], [jax.ShapeDtypeStruct((T, D_MODEL), F32)]
    if plan.proj_outs:
        p_g, p_w = proj[0], proj[1]
        in_specs += [_const_spec((1, D_MODEL)), _const_spec(p_w.shape)]
        args += [p_g.reshape(1, D_MODEL), p_w]
        heads = p_w.shape[1] // (plan.proj_outs * SB_HEAD_DIM)
        tiles_per_seq = seq_len // tm
        out_specs += [pl.BlockSpec((1, heads, tm, SB_HEAD_DIM),
                                   lambda i: (i // tiles_per_seq, 0, i % tiles_per_seq, 0))
                      ] * plan.proj_outs
        out_shape += [jax.ShapeDtypeStruct((T // seq_len, heads, seq_len, SB_HEAD_DIM), BF16)
                      ] * plan.proj_outs
    return pl.pallas_call(
        functools.partial(_ffn_kernel, plan=plan),
        grid=(T // tm,),
        in_specs=in_specs,
        out_specs=out_specs,
        out_shape=out_shape,
        compiler_params=_params(1),
        name="ffn_l%d_h%d" % (layer, half),
    )(*args)


def _sgu_kernel(x_ref, g_ref, win_ref, lng_ref, lnb_ref, ws_ref, bs_ref, wout_ref, o_ref):
    x = x_ref[...]
    tm = x.shape[0]
    h = _rms_norm(x, g_ref[...]).astype(BF16)
    def in_proj(c0):
        return jnp.dot(h, win_ref[:, c0:c0 + SGU_GROUP_DIM], preferred_element_type=F32)

    v = jnp.concatenate([_gelu_tanh(in_proj(D_A + gi * SGU_GROUP_DIM))
                         for gi in range(SGU_GROUPS)], axis=1)
    u_raw = [in_proj(gi * SGU_GROUP_DIM) for gi in range(SGU_GROUPS)]
    mu = jnp.mean(v, axis=-1, keepdims=True)
    vc = v - mu
    var = jnp.mean(vc * vc, axis=-1, keepdims=True)
    vn = (vc * lax.rsqrt(var + LN_EPS) * lng_ref[...] + lnb_ref[...]).astype(BF16)

    row = lax.broadcasted_iota(jnp.int32, (CHUNK, CHUNK), 0)
    col = lax.broadcasted_iota(jnp.int32, (CHUNK, CHUNK), 1)
    causal = row >= col
    bs = bs_ref[...]

    def spatial(gi):
        ws = jnp.where(causal, ws_ref[gi], 0.0).astype(BF16)
        bias = bs[:, gi:gi + 1]
        c0 = gi * SGU_GROUP_DIM
        return jnp.concatenate(
            [jnp.dot(ws, vn[n * CHUNK:(n + 1) * CHUNK, c0:c0 + SGU_GROUP_DIM],
                     preferred_element_type=F32) + bias for n in range(tm // CHUNK)], axis=0)

    acc = jnp.zeros(x.shape, F32)
    vs_next = spatial(0)
    for gi in range(SGU_GROUPS):
        vs = vs_next
        if gi + 1 < SGU_GROUPS:
            vs_next = spatial(gi + 1)
        gated = (_gelu_tanh(u_raw[gi]) * vs).astype(BF16)
        acc = acc + jnp.dot(gated, wout_ref[gi * SGU_GROUP_DIM:(gi + 1) * SGU_GROUP_DIM, :],
                            preferred_element_type=F32)
    o_ref[...] = x + acc


def _sgu(x2, g, w_in, ln_g, ln_b, w_s, b_s, w_out):
    T = x2.shape[0]
    tm = TOKEN_TILE
    row = pl.BlockSpec((tm, D_MODEL), lambda i: (i, 0))
    return pl.pallas_call(
        _sgu_kernel,
        grid=(T // tm,),
        in_specs=[row, _const_spec((1, D_MODEL)), _const_spec(w_in.shape),
                  _const_spec((1, D_A)), _const_spec((1, D_A)),
                  _const_spec(w_s.shape), _const_spec((CHUNK, SGU_GROUPS)),
                  _const_spec(w_out.shape)],
        out_specs=row,
        out_shape=jax.ShapeDtypeStruct((T, D_MODEL), F32),
        compiler_params=_params(1),
        name="sgu",
    )(x2, g.reshape(1, D_MODEL), w_in, ln_g.reshape(1, D_A), ln_b.reshape(1, D_A),
      w_s, b_s.T, w_out)


def _attn_kernel(q_ref, k_ref, v_ref, o_ref, acc_ref, run_ref):
    qt = pl.program_id(2)
    blk = ATT_BLOCK
    nsub = ATT_Q_SUB
    row = lax.broadcasted_iota(jnp.int32, (blk, blk), 0)
    col = lax.broadcasted_iota(jnp.int32, (blk, blk), 1)
    suffix = jnp.where(row > col, -1.0, 0.0).astype(BF16)
    strictly_earlier = col < row

    def load_kv(j):
        rows = pl.ds(pl.multiple_of(jnp.maximum(j, 0) * blk, blk), blk)
        return k_ref[0, 0, rows, :], v_ref[0, 0, rows, :]

    def finished_after(j):
        return jnp.where(j >= 1, 0.0, FINISHED_RUN).astype(F32)

    def scores(unit):
        sub, hd, (kb, _), diag, retire = unit
        idx = sub * SB_GROUP + hd
        z = lax.dot_general(q_ref[0, hd, sub * blk:(sub + 1) * blk, :], kb,
                            (((1,), (1,)), ((), ())), preferred_element_type=F32)
        sp = jnp.maximum(z, jnp.log2(1.0 + jnp.exp2(jnp.minimum(z, SOFTPLUS_CLAMP))))
        if diag:
            sp = jnp.where(strictly_earlier, sp, 0.0)
        run = 0.0 if diag else run_ref[idx]
        new_run = run - jnp.broadcast_to(jnp.sum(sp, axis=-1, keepdims=True), (blk, 128))
        if retire is not None:
            new_run = new_run + retire
        run_ref[idx] = new_run
        w = z - sp
        if not diag:
            w = w + jnp.concatenate([run, run], axis=1)
        return sp.astype(BF16), w, new_run

    def weights(unit, st):
        sp, w = st
        a = jnp.exp2(w + jnp.dot(sp, suffix, preferred_element_type=F32))
        if unit[3]:
            a = jnp.where(strictly_earlier, a, 0.0)
        return a.astype(BF16)

    def accumulate(unit, a):
        sub, hd, (_, vb), diag, _ = unit
        idx = sub * SB_GROUP + hd
        pv = jnp.dot(a, vb, preferred_element_type=F32)
        if diag:
            acc_ref[idx] = pv
        else:
            acc_ref[idx] += pv

    def run_units(units, tracked):
        n = len(units)
        st, a = {}, {}
        max_run = None
        for step in range(n + 2):
            if step < n:
                sp, w, new_run = scores(units[step])
                st[step] = (sp, w)
                if step in tracked:
                    max_run = new_run if max_run is None else jnp.maximum(max_run, new_run)
            if 0 <= step - 1 < n:
                a[step - 1] = weights(units[step - 1], st.pop(step - 1))
            if step - 2 >= 0:
                accumulate(units[step - 2], a.pop(step - 2))
        return max_run

    def alive(max_run):
        return (jnp.max(max_run) > DEAD_RUN).astype(jnp.int32)

    first = qt * nsub
    units, tracked = [], set()
    for off in range(nsub - 1, -2, -1):
        j = first + off
        kv = load_kv(j)
        retire = finished_after(j) if off <= 0 else None
        for sub in (off, off + 1):
            if 0 <= sub < nsub:
                for hd in range(SB_GROUP):
                    if sub == off + 1:
                        tracked.add(len(units))
                    units.append((sub, hd, kv, sub == off, retire))
    head_max = run_units(units, tracked)

    n_rounds = first + nsub - 2

    def more_rounds(carry):
        i, live = carry
        return jnp.logical_and(i < n_rounds, live > 0)

    def one_round(carry):
        i, _ = carry
        units = []
        for sub in range(nsub):
            j = first + sub - 2 - i
            kv = load_kv(j)
            retire = finished_after(j)
            units += [(sub, hd, kv, False, retire) for hd in range(SB_GROUP)]
        return i + 1, alive(run_units(units, set(range(len(units)))))

    lax.while_loop(more_rounds, one_round, (jnp.int32(0), alive(head_max)))

    for sub in range(nsub):
        o_ref[0, sub * blk:(sub + 1) * blk, :] = jnp.concatenate(
            [acc_ref[sub * SB_GROUP + hd] for hd in range(SB_GROUP)], axis=1).astype(o_ref.dtype)


def _attention(q, k, v):
    B, _, S, _ = q.shape
    tq = ATT_BLOCK * ATT_Q_SUB
    kv_spec = pl.BlockSpec((1, 1, S, SB_HEAD_DIM), lambda b, kh, i: (b, kh, 0, 0))
    return pl.pallas_call(
        _attn_kernel,
        grid=(B, SB_KV_HEADS, S // tq),
        in_specs=[pl.BlockSpec((1, SB_GROUP, tq, SB_HEAD_DIM), lambda b, kh, i: (b, kh, i, 0)),
                  kv_spec, kv_spec],
        out_specs=pl.BlockSpec((1, tq, SB_GROUP * SB_HEAD_DIM), lambda b, kh, i: (b, i, kh)),
        out_shape=jax.ShapeDtypeStruct((B, S, SB_HEADS * SB_HEAD_DIM), BF16),
        scratch_shapes=[pltpu.VMEM((ATT_Q_SUB * SB_GROUP, ATT_BLOCK, SB_HEAD_DIM), F32),
                        pltpu.VMEM((ATT_Q_SUB * SB_GROUP, ATT_BLOCK, 128), F32)],
        compiler_params=_params(3),
        name="sb_attention",
    )(q, k, v)


def kernel(x, ffn_norm, ffn_w_in, ffn_w_out, mix_norm, a_w_in, a_ln_g, a_ln_b, a_w_s, a_b_s,
           a_w_out, kv_norm, w_kv, b_w_q, b_w_o, final_norm):
    B, S, D = x.shape
    depth = ffn_norm.shape[0]
    n_a = a_w_in.shape[0]
    assert 1 <= n_a < depth, "shared K/V are produced by the last mixer-A layer"
    w_in = ffn_w_in.astype(BF16)
    w_out = ffn_w_out.astype(BF16)
    ffn = functools.partial(_ffn, seq_len=S, norms=ffn_norm, w_in=w_in, w_out=w_out)
    q_scale = LOG2E / math.sqrt(SB_HEAD_DIM)
    x2 = x.reshape(B * S, D)
    k = v = None
    for i in range(depth):
        if i < n_a:
            (x2,) = ffn(x2, layer=i, half=0)
            x2 = _sgu(x2, mix_norm[i], a_w_in[i].astype(BF16), a_ln_g[i], a_ln_b[i],
                      a_w_s[i], a_b_s[i], a_w_out[i].astype(BF16))
            if i == n_a - 1:
                x2, k, v = ffn(x2, layer=i, half=1, proj=(kv_norm, w_kv.astype(BF16), 2, 1.0))
            else:
                (x2,) = ffn(x2, layer=i, half=1)
        else:
            j = i - n_a
            x2, q = ffn(x2, layer=i, half=0,
                        proj=(mix_norm[i], b_w_q[j].astype(BF16), 1, q_scale))
            o = _attention(q, k, v)
            (x2,) = ffn(x2, layer=i, half=1, attn=(o.reshape(B * S, D), b_w_o[j].astype(BF16)),
                        final_g=final_norm if i == depth - 1 else None)
    return x2.reshape(B, S, D)
```

```python
import functools
import math
from typing import NamedTuple

import jax
import jax.numpy as jnp
from jax import lax
from jax.experimental import pallas as pl
from jax.experimental.pallas import tpu as pltpu

D_MODEL = 1024
D_FF = 2816
D_A = 2 * D_MODEL
CHUNK = 128
SGU_GROUPS = 8
SGU_GROUP_DIM = D_A // SGU_GROUPS
SB_HEADS = 16
SB_HEAD_DIM = 64
SB_KV_HEADS = 4
SB_GROUP = SB_HEADS // SB_KV_HEADS
FFN_RES_SCALE = 0.5
NORM_EPS = 1e-6
LN_EPS = 1e-5
LOG2E = 1.4426950408889634
SOFTPLUS_CLAMP = 64.0
DEAD_RUN = -160.0
FINISHED_RUN = -1e30

VMEM_LIMIT_BYTES = 56 * 1024 * 1024

BF16_SUBLANE_TILE = 16
STREAM_CHUNK_BYTES = 3 * 512 * 1024

TOKEN_TILE = 512
FFN_CHUNK = 256
ATT_BLOCK = 256
ATT_Q_SUB = 4

BF16 = jnp.bfloat16
F32 = jnp.float32


def _const_spec(shape):
    nd = len(shape)
    return pl.BlockSpec(shape, lambda *_: (0,) * nd, pipeline_mode=pl.Buffered(1))


def _rms_norm(x, g):
    return x * lax.rsqrt(jnp.mean(x * x, axis=-1, keepdims=True) + NORM_EPS) * g


def _gelu_tanh(x):
    c1 = math.sqrt(2.0 / math.pi)
    c2 = c1 * 0.044715
    return x * (0.5 + 0.5 * jnp.tanh(x * (c1 + c2 * (x * x))))


def _params(n_grid_dims, semantics="parallel"):
    return pltpu.CompilerParams(
        dimension_semantics=(semantics,) * n_grid_dims,
        vmem_limit_bytes=VMEM_LIMIT_BYTES)


HBM_SPEC = pl.BlockSpec(memory_space=pl.ANY)


def _chunk_rows(rows, cols):
    limit = STREAM_CHUNK_BYTES // (4 * cols)
    fits = [r for r in range(BF16_SUBLANE_TILE, rows + 1, BF16_SUBLANE_TILE)
            if rows % r == 0 and r <= limit]
    return fits[-1]


def _resident_weight_scratch(shape):
    rows, cols = shape
    return [pltpu.VMEM((rows, cols), BF16),
            pltpu.VMEM((2, _chunk_rows(rows, cols), cols), F32),
            pltpu.SemaphoreType.DMA((2,))]


def _stream_cast(src, dst, stage, sem):
    rc = stage.shape[1]
    n = src.shape[0] // rc

    def chunk_copy(c, slot):
        return pltpu.make_async_copy(src.at[pl.ds(c * rc, rc)], stage.at[slot], sem.at[slot])

    chunk_copy(0, 0).start()

    def body(c, carry):
        slot = lax.rem(c, 2)

        @pl.when(c + 1 < n)
        def _():
            chunk_copy(c + 1, 1 - slot).start()

        chunk_copy(c, slot).wait()
        dst[pl.ds(pl.multiple_of(c * rc, rc), rc), :] = stage[slot].astype(BF16)
        return carry

    lax.fori_loop(0, n, body, 0)


class _FfnPlan(NamedTuple):
    layer: int
    half: int
    attn_in: bool
    final_norm: bool
    proj_outs: int
    proj_scale: float


def _ffn_kernel(*refs, plan):
    it = iter(refs)
    x_ref = next(it)
    if plan.attn_in:
        o_ref, wo_hbm = next(it), next(it)
    g_ref, win_hbm, wout_hbm = next(it), next(it), next(it)
    if plan.final_norm:
        fg_ref = next(it)
    if plan.proj_outs:
        pg_ref, pw_hbm = next(it), next(it)
    y_ref = next(it)
    proj_refs = [next(it) for _ in range(plan.proj_outs)]
    win_ref, win_stage, win_sem = next(it), next(it), next(it)
    wout_ref, wout_stage, wout_sem = next(it), next(it), next(it)
    streams = [(win_hbm.at[plan.layer, plan.half], win_ref, win_stage, win_sem),
               (wout_hbm.at[plan.layer, plan.half], wout_ref, wout_stage, wout_sem)]
    if plan.attn_in:
        wo_ref, wo_stage, wo_sem = next(it), next(it), next(it)
        streams.insert(0, (wo_hbm, wo_ref, wo_stage, wo_sem))
    if plan.proj_outs:
        pw_ref, pw_stage, pw_sem = next(it), next(it), next(it)
        streams.append((pw_hbm, pw_ref, pw_stage, pw_sem))

    @pl.when(pl.program_id(0) == 0)
    def _():
        for stream in streams:
            _stream_cast(*stream)

    x = x_ref[...]
    if plan.attn_in:
        x = x + jnp.dot(o_ref[...], wo_ref[...], preferred_element_type=F32)
    h = _rms_norm(x, g_ref[...]).astype(BF16)
    acc = jnp.zeros(x.shape, F32)
    for c in range(D_FF // FFN_CHUNK):
        lo = c * FFN_CHUNK
        gate = jnp.dot(h, win_ref[:, lo:lo + FFN_CHUNK], preferred_element_type=F32)
        up = jnp.dot(h, win_ref[:, D_FF + lo:D_FF + lo + FFN_CHUNK], preferred_element_type=F32)
        act = (gate * jax.nn.sigmoid(gate) * up).astype(BF16)
        acc = acc + jnp.dot(act, wout_ref[lo:lo + FFN_CHUNK, :], preferred_element_type=F32)
    y = x + FFN_RES_SCALE * acc
    if plan.proj_outs:
        p = jnp.dot(_rms_norm(y, pg_ref[...]).astype(BF16), pw_ref[...],
                    preferred_element_type=F32)
        if plan.proj_scale != 1.0:
            p = p * plan.proj_scale
        p = p.astype(BF16)
        heads = proj_refs[0].shape[1]
        for oi, p_ref in enumerate(proj_refs):
            for hd in range(heads):
                c0 = (oi * heads + hd) * SB_HEAD_DIM
                p_ref[0, hd] = p[:, c0:c0 + SB_HEAD_DIM]
    if plan.final_norm:
        y = _rms_norm(y, fg_ref[...])
    y_ref[...] = y


def _ffn(x2, seq_len, norms, w_in, w_out, layer, half, *, attn=None, final_g=None, proj=None):
    T = x2.shape[0]
    tm = TOKEN_TILE
    plan = _FfnPlan(layer, half, attn is not None, final_g is not None,
                    0 if proj is None else proj[2], 1.0 if proj is None else proj[3])
    row = pl.BlockSpec((tm, D_MODEL), lambda i: (i, 0))
    gain_spec = pl.BlockSpec((None, None, 1, D_MODEL), lambda i: (layer, half, 0, 0),
                             pipeline_mode=pl.Buffered(1))

    in_specs, args = [row], [x2]
    scratch = _resident_weight_scratch(w_in.shape[2:]) + _resident_weight_scratch(w_out.shape[2:])
    if plan.attn_in:
        o2, w_o = attn
        in_specs += [row, HBM_SPEC]
        args += [o2, w_o]
        scratch += _resident_weight_scratch(w_o.shape)
    in_specs += [gain_spec, HBM_SPEC, HBM_SPEC]
    args += [norms.reshape(norms.shape[0], 2, 1, D_MODEL), w_in, w_out]
    if plan.final_norm:
        in_specs.append(_const_spec((1, D_MODEL)))
        args.append(final_g.reshape(1, D_MODEL))
    out_specs, out_shape = [row], [jax.ShapeDtypeStruct((T, D_MODEL), F32)]
    if plan.proj_outs:
        p_g, p_w = proj[0], proj[1]
        in_specs += [_const_spec((1, D_MODEL)), HBM_SPEC]
        args += [p_g.reshape(1, D_MODEL), p_w]
        scratch += _resident_weight_scratch(p_w.shape)
        heads = p_w.shape[1] // (plan.proj_outs * SB_HEAD_DIM)
        tiles_per_seq = seq_len // tm
        out_specs += [pl.BlockSpec((1, heads, tm, SB_HEAD_DIM),
                                   lambda i: (i // tiles_per_seq, 0, i % tiles_per_seq, 0))
                      ] * plan.proj_outs
        out_shape += [jax.ShapeDtypeStruct((T // seq_len, heads, seq_len, SB_HEAD_DIM), BF16)
                      ] * plan.proj_outs
    return pl.pallas_call(
        functools.partial(_ffn_kernel, plan=plan),
        grid=(T // tm,),
        in_specs=in_specs,
        out_specs=out_specs,
        out_shape=out_shape,
        scratch_shapes=scratch,
        compiler_params=_params(1, "arbitrary"),
        name="ffn_l%d_h%d" % (layer, half),
    )(*args)


def _sgu_kernel(x_ref, g_ref, win_hbm, lng_ref, lnb_ref, ws_ref, bs_ref, wout_hbm, o_ref,
                win_ref, win_stage, win_sem, wout_ref, wout_stage, wout_sem):
    @pl.when(pl.program_id(0) == 0)
    def _():
        _stream_cast(win_hbm, win_ref, win_stage, win_sem)
        _stream_cast(wout_hbm, wout_ref, wout_stage, wout_sem)

    x = x_ref[...]
    tm = x.shape[0]
    h = _rms_norm(x, g_ref[...]).astype(BF16)
    def in_proj(c0):
        return jnp.dot(h, win_ref[:, c0:c0 + SGU_GROUP_DIM], preferred_element_type=F32)

    v = jnp.concatenate([_gelu_tanh(in_proj(D_A + gi * SGU_GROUP_DIM))
                         for gi in range(SGU_GROUPS)], axis=1)
    u_raw = [in_proj(gi * SGU_GROUP_DIM) for gi in range(SGU_GROUPS)]
    mu = jnp.mean(v, axis=-1, keepdims=True)
    vc = v - mu
    var = jnp.mean(vc * vc, axis=-1, keepdims=True)
    vn = (vc * lax.rsqrt(var + LN_EPS) * lng_ref[...] + lnb_ref[...]).astype(BF16)

    row = lax.broadcasted_iota(jnp.int32, (CHUNK, CHUNK), 0)
    col = lax.broadcasted_iota(jnp.int32, (CHUNK, CHUNK), 1)
    causal = row >= col
    bs = bs_ref[...]

    def spatial(gi):
        ws = jnp.where(causal, ws_ref[gi], 0.0).astype(BF16)
        bias = bs[:, gi:gi + 1]
        c0 = gi * SGU_GROUP_DIM
        return jnp.concatenate(
            [jnp.dot(ws, vn[n * CHUNK:(n + 1) * CHUNK, c0:c0 + SGU_GROUP_DIM],
                     preferred_element_type=F32) + bias for n in range(tm // CHUNK)], axis=0)

    acc = jnp.zeros(x.shape, F32)
    vs_next = spatial(0)
    for gi in range(SGU_GROUPS):
        vs = vs_next
        if gi + 1 < SGU_GROUPS:
            vs_next = spatial(gi + 1)
        gated = (_gelu_tanh(u_raw[gi]) * vs).astype(BF16)
        acc = acc + jnp.dot(gated, wout_ref[gi * SGU_GROUP_DIM:(gi + 1) * SGU_GROUP_DIM, :],
                            preferred_element_type=F32)
    o_ref[...] = x + acc


def _sgu(x2, g, w_in, ln_g, ln_b, w_s, b_s, w_out):
    T = x2.shape[0]
    tm = TOKEN_TILE
    row = pl.BlockSpec((tm, D_MODEL), lambda i: (i, 0))
    return pl.pallas_call(
        _sgu_kernel,
        grid=(T // tm,),
        in_specs=[row, _const_spec((1, D_MODEL)), HBM_SPEC,
                  _const_spec((1, D_A)), _const_spec((1, D_A)),
                  _const_spec(w_s.shape), _const_spec((CHUNK, SGU_GROUPS)),
                  HBM_SPEC],
        out_specs=row,
        out_shape=jax.ShapeDtypeStruct((T, D_MODEL), F32),
        scratch_shapes=(_resident_weight_scratch(w_in.shape)
                        + _resident_weight_scratch(w_out.shape)),
        compiler_params=_params(1, "arbitrary"),
        name="sgu",
    )(x2, g.reshape(1, D_MODEL), w_in, ln_g.reshape(1, D_A), ln_b.reshape(1, D_A),
      w_s, b_s.T, w_out)


def _attn_kernel(q_ref, k_ref, v_ref, o_ref, acc_ref, run_ref):
    qt = pl.program_id(2)
    blk = ATT_BLOCK
    nsub = ATT_Q_SUB
    row = lax.broadcasted_iota(jnp.int32, (blk, blk), 0)
    col = lax.broadcasted_iota(jnp.int32, (blk, blk), 1)
    suffix = jnp.where(row > col, -1.0, 0.0).astype(BF16)
    strictly_earlier = col < row

    def load_kv(j):
        rows = pl.ds(pl.multiple_of(jnp.maximum(j, 0) * blk, blk), blk)
        return k_ref[0, 0, rows, :], v_ref[0, 0, rows, :]

    def finished_after(j):
        return jnp.where(j >= 1, 0.0, FINISHED_RUN).astype(F32)

    def scores(unit):
        sub, hd, (kb, _), diag, retire = unit
        idx = sub * SB_GROUP + hd
        z = lax.dot_general(q_ref[0, hd, sub * blk:(sub + 1) * blk, :], kb,
                            (((1,), (1,)), ((), ())), preferred_element_type=F32)
        sp = jnp.maximum(z, jnp.log2(1.0 + jnp.exp2(jnp.minimum(z, SOFTPLUS_CLAMP))))
        if diag:
            sp = jnp.where(strictly_earlier, sp, 0.0)
        run = 0.0 if diag else run_ref[idx]
        new_run = run - jnp.broadcast_to(jnp.sum(sp, axis=-1, keepdims=True), (blk, 128))
        if retire is not None:
            new_run = new_run + retire
        run_ref[idx] = new_run
        w = z - sp
        if not diag:
            w = w + jnp.concatenate([run, run], axis=1)
        return sp.astype(BF16), w, new_run

    def weights(unit, st):
        sp, w = st
        a = jnp.exp2(w + jnp.dot(sp, suffix, preferred_element_type=F32))
        if unit[3]:
            a = jnp.where(strictly_earlier, a, 0.0)
        return a.astype(BF16)

    def accumulate(unit, a):
        sub, hd, (_, vb), diag, _ = unit
        idx = sub * SB_GROUP + hd
        pv = jnp.dot(a, vb, preferred_element_type=F32)
        if diag:
            acc_ref[idx] = pv
        else:
            acc_ref[idx] += pv

    def run_units(units, tracked):
        n = len(units)
        st, a = {}, {}
        max_run = None
        for step in range(n + 2):
            if step < n:
                sp, w, new_run = scores(units[step])
                st[step] = (sp, w)
                if step in tracked:
                    max_run = new_run if max_run is None else jnp.maximum(max_run, new_run)
            if 0 <= step - 1 < n:
                a[step - 1] = weights(units[step - 1], st.pop(step - 1))
            if step - 2 >= 0:
                accumulate(units[step - 2], a.pop(step - 2))
        return max_run

    def alive(max_run):
        return (jnp.max(max_run) > DEAD_RUN).astype(jnp.int32)

    first = qt * nsub
    units, tracked = [], set()
    for off in range(nsub - 1, -2, -1):
        j = first + off
        kv = load_kv(j)
        retire = finished_after(j) if off <= 0 else None
        for sub in (off, off + 1):
            if 0 <= sub < nsub:
                for hd in range(SB_GROUP):
                    if sub == off + 1:
                        tracked.add(len(units))
                    units.append((sub, hd, kv, sub == off, retire))
    head_max = run_units(units, tracked)

    n_rounds = first + nsub - 2

    def more_rounds(carry):
        i, live = carry
        return jnp.logical_and(i < n_rounds, live > 0)

    def one_round(carry):
        i, _ = carry
        units = []
        for sub in range(nsub):
            j = first + sub - 2 - i
            kv = load_kv(j)
            retire = finished_after(j)
            units += [(sub, hd, kv, False, retire) for hd in range(SB_GROUP)]
        return i + 1, alive(run_units(units, set(range(len(units)))))

    lax.while_loop(more_rounds, one_round, (jnp.int32(0), alive(head_max)))

    for sub in range(nsub):
        o_ref[0, sub * blk:(sub + 1) * blk, :] = jnp.concatenate(
            [acc_ref[sub * SB_GROUP + hd] for hd in range(SB_GROUP)], axis=1).astype(o_ref.dtype)


def _attention(q, k, v):
    B, _, S, _ = q.shape
    tq = ATT_BLOCK * ATT_Q_SUB
    kv_spec = pl.BlockSpec((1, 1, S, SB_HEAD_DIM), lambda b, kh, i: (b, kh, 0, 0))
    return pl.pallas_call(
        _attn_kernel,
        grid=(B, SB_KV_HEADS, S // tq),
        in_specs=[pl.BlockSpec((1, SB_GROUP, tq, SB_HEAD_DIM), lambda b, kh, i: (b, kh, i, 0)),
                  kv_spec, kv_spec],
        out_specs=pl.BlockSpec((1, tq, SB_GROUP * SB_HEAD_DIM), lambda b, kh, i: (b, i, kh)),
        out_shape=jax.ShapeDtypeStruct((B, S, SB_HEADS * SB_HEAD_DIM), BF16),
        scratch_shapes=[pltpu.VMEM((ATT_Q_SUB * SB_GROUP, ATT_BLOCK, SB_HEAD_DIM), F32),
                        pltpu.VMEM((ATT_Q_SUB * SB_GROUP, ATT_BLOCK, 128), F32)],
        compiler_params=_params(3),
        name="sb_attention",
    )(q, k, v)


def kernel(x, ffn_norm, ffn_w_in, ffn_w_out, mix_norm, a_w_in, a_ln_g, a_ln_b, a_w_s, a_b_s,
           a_w_out, kv_norm, w_kv, b_w_q, b_w_o, final_norm):
    B, S, D = x.shape
    depth = ffn_norm.shape[0]
    n_a = a_w_in.shape[0]
    assert 1 <= n_a < depth, "shared K/V are produced by the last mixer-A layer"
    ffn = functools.partial(_ffn, seq_len=S, norms=ffn_norm, w_in=ffn_w_in, w_out=ffn_w_out)
    q_scale = LOG2E / math.sqrt(SB_HEAD_DIM)
    x2 = x.reshape(B * S, D)
    k = v = None
    for i in range(depth):
        if i < n_a:
            (x2,) = ffn(x2, layer=i, half=0)
            x2 = _sgu(x2, mix_norm[i], a_w_in[i], a_ln_g[i], a_ln_b[i], a_w_s[i], a_b_s[i],
                      a_w_out[i])
            if i == n_a - 1:
                x2, k, v = ffn(x2, layer=i, half=1, proj=(kv_norm, w_kv, 2, 1.0))
            else:
                (x2,) = ffn(x2, layer=i, half=1)
        else:
            j = i - n_a
            x2, q = ffn(x2, layer=i, half=0, proj=(mix_norm[i], b_w_q[j], 1, q_scale))
            o = _attention(q, k, v)
            (x2,) = ffn(x2, layer=i, half=1, attn=(o.reshape(B * S, D), b_w_o[j]),
                        final_g=final_norm if i == depth - 1 else None)
    return x2.reshape(B, S, D)
```

```python
import functools
import math
from typing import NamedTuple

import jax
import jax.numpy as jnp
from jax import lax
from jax.experimental import pallas as pl
from jax.experimental.pallas import tpu as pltpu

D_MODEL = 1024
D_FF = 2816
D_A = 2 * D_MODEL
CHUNK = 128
SGU_GROUPS = 8
SGU_GROUP_DIM = D_A // SGU_GROUPS
SB_HEADS = 16
SB_HEAD_DIM = 64
SB_KV_HEADS = 4
SB_GROUP = SB_HEADS // SB_KV_HEADS
FFN_RES_SCALE = 0.5
NORM_EPS = 1e-6
LN_EPS = 1e-5
LOG2E = 1.4426950408889634
SOFTPLUS_CLAMP = 64.0
DEAD_RUN = -160.0
FINISHED_RUN = -1e30

VMEM_LIMIT_BYTES = 56 * 1024 * 1024

BF16_SUBLANE_TILE = 16
STREAM_CHUNK_BYTES = 3 * 512 * 1024
STREAM_SLOTS = 3

TOKEN_TILE = 512
FFN_CHUNK = 256
ATT_BLOCK = 256
ATT_Q_SUB = 4

BF16 = jnp.bfloat16
F32 = jnp.float32


def _const_spec(shape):
    nd = len(shape)
    return pl.BlockSpec(shape, lambda *_: (0,) * nd, pipeline_mode=pl.Buffered(1))


def _rms_norm(x, g):
    return x * lax.rsqrt(jnp.mean(x * x, axis=-1, keepdims=True) + NORM_EPS) * g


def _gelu_tanh(x):
    c1 = math.sqrt(2.0 / math.pi)
    c2 = c1 * 0.044715
    return x * (0.5 + 0.5 * jnp.tanh(x * (c1 + c2 * (x * x))))


def _params(n_grid_dims, semantics="parallel"):
    return pltpu.CompilerParams(
        dimension_semantics=(semantics,) * n_grid_dims,
        vmem_limit_bytes=VMEM_LIMIT_BYTES)


HBM_SPEC = pl.BlockSpec(memory_space=pl.ANY)


def _chunk_rows(rows, cols):
    limit = STREAM_CHUNK_BYTES // (4 * cols)
    fits = [r for r in range(BF16_SUBLANE_TILE, rows + 1, BF16_SUBLANE_TILE)
            if rows % r == 0 and r <= limit]
    return fits[-1]


def _resident_weight_scratch(shape):
    rows, cols = shape
    return [pltpu.VMEM((rows, cols), BF16),
            pltpu.VMEM((STREAM_SLOTS, _chunk_rows(rows, cols), cols), F32),
            pltpu.SemaphoreType.DMA((STREAM_SLOTS,))]


def _stream_cast(src, dst, stage, sem):
    rc = stage.shape[1]
    n = src.shape[0] // rc
    ahead = STREAM_SLOTS - 1

    def chunk_copy(c, slot):
        return pltpu.make_async_copy(src.at[pl.ds(c * rc, rc)], stage.at[slot], sem.at[slot])

    for c in range(min(ahead, n)):
        chunk_copy(c, c).start()

    def body(c, carry):
        slot = lax.rem(c, STREAM_SLOTS)

        @pl.when(c + ahead < n)
        def _():
            chunk_copy(c + ahead, lax.rem(c + ahead, STREAM_SLOTS)).start()

        chunk_copy(c, slot).wait()
        dst[pl.ds(pl.multiple_of(c * rc, rc), rc), :] = stage[slot].astype(BF16)
        return carry

    lax.fori_loop(0, n, body, 0)


class _FfnPlan(NamedTuple):
    layer: int
    half: int
    attn_in: bool
    final_norm: bool
    proj_outs: int
    proj_scale: float


def _ffn_kernel(*refs, plan):
    it = iter(refs)
    x_ref = next(it)
    if plan.attn_in:
        o_ref, wo_hbm = next(it), next(it)
    g_ref, win_hbm, wout_hbm = next(it), next(it), next(it)
    if plan.final_norm:
        fg_ref = next(it)
    if plan.proj_outs:
        pg_ref, pw_hbm = next(it), next(it)
    y_ref = next(it)
    proj_refs = [next(it) for _ in range(plan.proj_outs)]
    win_ref, win_stage, win_sem = next(it), next(it), next(it)
    wout_ref, wout_stage, wout_sem = next(it), next(it), next(it)
    streams = [(win_hbm.at[plan.layer, plan.half], win_ref, win_stage, win_sem),
               (wout_hbm.at[plan.layer, plan.half], wout_ref, wout_stage, wout_sem)]
    if plan.attn_in:
        wo_ref, wo_stage, wo_sem = next(it), next(it), next(it)
        streams.insert(0, (wo_hbm, wo_ref, wo_stage, wo_sem))
    if plan.proj_outs:
        pw_ref, pw_stage, pw_sem = next(it), next(it), next(it)
        streams.append((pw_hbm, pw_ref, pw_stage, pw_sem))

    @pl.when(pl.program_id(0) == 0)
    def _():
        for stream in streams:
            _stream_cast(*stream)

    x = x_ref[...]
    if plan.attn_in:
        x = x + jnp.dot(o_ref[...], wo_ref[...], preferred_element_type=F32)
    h = _rms_norm(x, g_ref[...]).astype(BF16)
    acc = jnp.zeros(x.shape, F32)
    for c in range(D_FF // FFN_CHUNK):
        lo = c * FFN_CHUNK
        gate = jnp.dot(h, win_ref[:, lo:lo + FFN_CHUNK], preferred_element_type=F32)
        up = jnp.dot(h, win_ref[:, D_FF + lo:D_FF + lo + FFN_CHUNK], preferred_element_type=F32)
        act = (gate * jax.nn.sigmoid(gate) * up).astype(BF16)
        acc = acc + jnp.dot(act, wout_ref[lo:lo + FFN_CHUNK, :], preferred_element_type=F32)
    y = x + FFN_RES_SCALE * acc
    if plan.proj_outs:
        p = jnp.dot(_rms_norm(y, pg_ref[...]).astype(BF16), pw_ref[...],
                    preferred_element_type=F32)
        if plan.proj_scale != 1.0:
            p = p * plan.proj_scale
        p = p.astype(BF16)
        heads = proj_refs[0].shape[1]
        for oi, p_ref in enumerate(proj_refs):
            for hd in range(heads):
                c0 = (oi * heads + hd) * SB_HEAD_DIM
                p_ref[0, hd] = p[:, c0:c0 + SB_HEAD_DIM]
    if plan.final_norm:
        y = _rms_norm(y, fg_ref[...])
    y_ref[...] = y


def _ffn(x2, seq_len, norms, w_in, w_out, layer, half, *, attn=None, final_g=None, proj=None):
    T = x2.shape[0]
    tm = TOKEN_TILE
    plan = _FfnPlan(layer, half, attn is not None, final_g is not None,
                    0 if proj is None else proj[2], 1.0 if proj is None else proj[3])
    row = pl.BlockSpec((tm, D_MODEL), lambda i: (i, 0))
    gain_spec = pl.BlockSpec((None, None, 1, D_MODEL), lambda i: (layer, half, 0, 0),
                             pipeline_mode=pl.Buffered(1))

    in_specs, args = [row], [x2]
    scratch = _resident_weight_scratch(w_in.shape[2:]) + _resident_weight_scratch(w_out.shape[2:])
    if plan.attn_in:
        o2, w_o = attn
        in_specs += [row, HBM_SPEC]
        args += [o2, w_o]
        scratch += _resident_weight_scratch(w_o.shape)
    in_specs += [gain_spec, HBM_SPEC, HBM_SPEC]
    args += [norms.reshape(norms.shape[0], 2, 1, D_MODEL), w_in, w_out]
    if plan.final_norm:
        in_specs.append(_const_spec((1, D_MODEL)))
        args.append(final_g.reshape(1, D_MODEL))
    out_specs, out_shape = [row], [jax.ShapeDtypeStruct((T, D_MODEL), F32)]
    if plan.proj_outs:
        p_g, p_w = proj[0], proj[1]
        in_specs += [_const_spec((1, D_MODEL)), HBM_SPEC]
        args += [p_g.reshape(1, D_MODEL), p_w]
        scratch += _resident_weight_scratch(p_w.shape)
        heads = p_w.shape[1] // (plan.proj_outs * SB_HEAD_DIM)
        tiles_per_seq = seq_len // tm
        out_specs += [pl.BlockSpec((1, heads, tm, SB_HEAD_DIM),
                                   lambda i: (i // tiles_per_seq, 0, i % tiles_per_seq, 0))
                      ] * plan.proj_outs
        out_shape += [jax.ShapeDtypeStruct((T // seq_len, heads, seq_len, SB_HEAD_DIM), BF16)
                      ] * plan.proj_outs
    return pl.pallas_call(
        functools.partial(_ffn_kernel, plan=plan),
        grid=(T // tm,),
        in_specs=in_specs,
        out_specs=out_specs,
        out_shape=out_shape,
        scratch_shapes=scratch,
        compiler_params=_params(1, "arbitrary"),
        name="ffn_l%d_h%d" % (layer, half),
    )(*args)


def _sgu_kernel(x_ref, g_ref, win_hbm, lng_ref, lnb_ref, ws_ref, bs_ref, wout_hbm, o_ref,
                win_ref, win_stage, win_sem, wout_ref, wout_stage, wout_sem):
    @pl.when(pl.program_id(0) == 0)
    def _():
        _stream_cast(win_hbm, win_ref, win_stage, win_sem)
        _stream_cast(wout_hbm, wout_ref, wout_stage, wout_sem)

    x = x_ref[...]
    tm = x.shape[0]
    h = _rms_norm(x, g_ref[...]).astype(BF16)
    def in_proj(c0):
        return jnp.dot(h, win_ref[:, c0:c0 + SGU_GROUP_DIM], preferred_element_type=F32)

    v = jnp.concatenate([_gelu_tanh(in_proj(D_A + gi * SGU_GROUP_DIM))
                         for gi in range(SGU_GROUPS)], axis=1)
    u_raw = [in_proj(gi * SGU_GROUP_DIM) for gi in range(SGU_GROUPS)]
    mu = jnp.mean(v, axis=-1, keepdims=True)
    vc = v - mu
    var = jnp.mean(vc * vc, axis=-1, keepdims=True)
    vn = (vc * lax.rsqrt(var + LN_EPS) * lng_ref[...] + lnb_ref[...]).astype(BF16)

    row = lax.broadcasted_iota(jnp.int32, (CHUNK, CHUNK), 0)
    col = lax.broadcasted_iota(jnp.int32, (CHUNK, CHUNK), 1)
    causal = row >= col
    bs = bs_ref[...]

    def spatial(gi):
        ws = jnp.where(causal, ws_ref[gi], 0.0).astype(BF16)
        bias = bs[:, gi:gi + 1]
        c0 = gi * SGU_GROUP_DIM
        return jnp.concatenate(
            [jnp.dot(ws, vn[n * CHUNK:(n + 1) * CHUNK, c0:c0 + SGU_GROUP_DIM],
                     preferred_element_type=F32) + bias for n in range(tm // CHUNK)], axis=0)

    acc = jnp.zeros(x.shape, F32)
    vs_next = spatial(0)
    for gi in range(SGU_GROUPS):
        vs = vs_next
        if gi + 1 < SGU_GROUPS:
            vs_next = spatial(gi + 1)
        gated = (_gelu_tanh(u_raw[gi]) * vs).astype(BF16)
        acc = acc + jnp.dot(gated, wout_ref[gi * SGU_GROUP_DIM:(gi + 1) * SGU_GROUP_DIM, :],
                            preferred_element_type=F32)
    o_ref[...] = x + acc


def _sgu(x2, g, w_in, ln_g, ln_b, w_s, b_s, w_out):
    T = x2.shape[0]
    tm = TOKEN_TILE
    row = pl.BlockSpec((tm, D_MODEL), lambda i: (i, 0))
    return pl.pallas_call(
        _sgu_kernel,
        grid=(T // tm,),
        in_specs=[row, _const_spec((1, D_MODEL)), HBM_SPEC,
                  _const_spec((1, D_A)), _const_spec((1, D_A)),
                  _const_spec(w_s.shape), _const_spec((CHUNK, SGU_GROUPS)),
                  HBM_SPEC],
        out_specs=row,
        out_shape=jax.ShapeDtypeStruct((T, D_MODEL), F32),
        scratch_shapes=(_resident_weight_scratch(w_in.shape)
                        + _resident_weight_scratch(w_out.shape)),
        compiler_params=_params(1, "arbitrary"),
        name="sgu",
    )(x2, g.reshape(1, D_MODEL), w_in, ln_g.reshape(1, D_A), ln_b.reshape(1, D_A),
      w_s, b_s.T, w_out)


def _attn_kernel(q_ref, k_ref, v_ref, o_ref, acc_ref, run_ref):
    qt = pl.program_id(2)
    blk = ATT_BLOCK
    nsub = ATT_Q_SUB
    row = lax.broadcasted_iota(jnp.int32, (blk, blk), 0)
    col = lax.broadcasted_iota(jnp.int32, (blk, blk), 1)
    suffix = jnp.where(row > col, -1.0, 0.0).astype(BF16)
    strictly_earlier = col < row

    def load_kv(j):
        rows = pl.ds(pl.multiple_of(jnp.maximum(j, 0) * blk, blk), blk)
        return k_ref[0, 0, rows, :], v_ref[0, 0, rows, :]

    def finished_after(j):
        return jnp.where(j >= 1, 0.0, FINISHED_RUN).astype(F32)

    def scores(unit):
        sub, hd, (kb, _), diag, retire = unit
        idx = sub * SB_GROUP + hd
        z = lax.dot_general(q_ref[0, hd, sub * blk:(sub + 1) * blk, :], kb,
                            (((1,), (1,)), ((), ())), preferred_element_type=F32)
        sp = jnp.maximum(z, jnp.log2(1.0 + jnp.exp2(jnp.minimum(z, SOFTPLUS_CLAMP))))
        if diag:
            sp = jnp.where(strictly_earlier, sp, 0.0)
        run = 0.0 if diag else run_ref[idx]
        new_run = run - jnp.broadcast_to(jnp.sum(sp, axis=-1, keepdims=True), (blk, 128))
        if retire is not None:
            new_run = new_run + retire
        run_ref[idx] = new_run
        w = z - sp
        if not diag:
            w = w + jnp.concatenate([run, run], axis=1)
        return sp.astype(BF16), w, new_run

    def weights(unit, st):
        sp, w = st
        a = jnp.exp2(w + jnp.dot(sp, suffix, preferred_element_type=F32))
        if unit[3]:
            a = jnp.where(strictly_earlier, a, 0.0)
        return a.astype(BF16)

    def accumulate(unit, a):
        sub, hd, (_, vb), diag, _ = unit
        idx = sub * SB_GROUP + hd
        pv = jnp.dot(a, vb, preferred_element_type=F32)
        if diag:
            acc_ref[idx] = pv
        else:
            acc_ref[idx] += pv

    def run_units(units, tracked):
        n = len(units)
        st, a = {}, {}
        max_run = None
        for step in range(n + 2):
            if step < n:
                sp, w, new_run = scores(units[step])
                st[step] = (sp, w)
                if step in tracked:
                    max_run = new_run if max_run is None else jnp.maximum(max_run, new_run)
            if 0 <= step - 1 < n:
                a[step - 1] = weights(units[step - 1], st.pop(step - 1))
            if step - 2 >= 0:
                accumulate(units[step - 2], a.pop(step - 2))
        return max_run

    def alive(max_run):
        return (jnp.max(max_run) > DEAD_RUN).astype(jnp.int32)

    first = qt * nsub
    units, tracked = [], set()
    for off in range(nsub - 1, -2, -1):
        j = first + off
        kv = load_kv(j)
        retire = finished_after(j) if off <= 0 else None
        for sub in (off, off + 1):
            if 0 <= sub < nsub:
                for hd in range(SB_GROUP):
                    if sub == off + 1:
                        tracked.add(len(units))
                    units.append((sub, hd, kv, sub == off, retire))
    head_max = run_units(units, tracked)

    n_rounds = first + nsub - 2

    def more_rounds(carry):
        i, live = carry
        return jnp.logical_and(i < n_rounds, live > 0)

    def one_round(carry):
        i, _ = carry
        units = []
        for sub in range(nsub):
            j = first + sub - 2 - i
            kv = load_kv(j)
            retire = finished_after(j)
            units += [(sub, hd, kv, False, retire) for hd in range(SB_GROUP)]
        return i + 1, alive(run_units(units, set(range(len(units)))))

    lax.while_loop(more_rounds, one_round, (jnp.int32(0), alive(head_max)))

    for sub in range(nsub):
        o_ref[0, sub * blk:(sub + 1) * blk, :] = jnp.concatenate(
            [acc_ref[sub * SB_GROUP + hd] for hd in range(SB_GROUP)], axis=1).astype(o_ref.dtype)


def _attention(q, k, v):
    B, _, S, _ = q.shape
    tq = ATT_BLOCK * ATT_Q_SUB
    kv_spec = pl.BlockSpec((1, 1, S, SB_HEAD_DIM), lambda b, kh, i: (b, kh, 0, 0))
    return pl.pallas_call(
        _attn_kernel,
        grid=(B, SB_KV_HEADS, S // tq),
        in_specs=[pl.BlockSpec((1, SB_GROUP, tq, SB_HEAD_DIM), lambda b, kh, i: (b, kh, i, 0)),
                  kv_spec, kv_spec],
        out_specs=pl.BlockSpec((1, tq, SB_GROUP * SB_HEAD_DIM), lambda b, kh, i: (b, i, kh)),
        out_shape=jax.ShapeDtypeStruct((B, S, SB_HEADS * SB_HEAD_DIM), BF16),
        scratch_shapes=[pltpu.VMEM((ATT_Q_SUB * SB_GROUP, ATT_BLOCK, SB_HEAD_DIM), F32),
                        pltpu.VMEM((ATT_Q_SUB * SB_GROUP, ATT_BLOCK, 128), F32)],
        compiler_params=_params(3),
        name="sb_attention",
    )(q, k, v)


def kernel(x, ffn_norm, ffn_w_in, ffn_w_out, mix_norm, a_w_in, a_ln_g, a_ln_b, a_w_s, a_b_s,
           a_w_out, kv_norm, w_kv, b_w_q, b_w_o, final_norm):
    B, S, D = x.shape
    depth = ffn_norm.shape[0]
    n_a = a_w_in.shape[0]
    assert 1 <= n_a < depth, "shared K/V are produced by the last mixer-A layer"
    ffn = functools.partial(_ffn, seq_len=S, norms=ffn_norm, w_in=ffn_w_in, w_out=ffn_w_out)
    q_scale = LOG2E / math.sqrt(SB_HEAD_DIM)
    x2 = x.reshape(B * S, D)
    k = v = None
    for i in range(depth):
        if i < n_a:
            (x2,) = ffn(x2, layer=i, half=0)
            x2 = _sgu(x2, mix_norm[i], a_w_in[i], a_ln_g[i], a_ln_b[i], a_w_s[i], a_b_s[i],
                      a_w_out[i])
            if i == n_a - 1:
                x2, k, v = ffn(x2, layer=i, half=1, proj=(kv_norm, w_kv, 2, 1.0))
            else:
                (x2,) = ffn(x2, layer=i, half=1)
        else:
            j = i - n_a
            x2, q = ffn(x2, layer=i, half=0, proj=(mix_norm[i], b_w_q[j], 1, q_scale))
            o = _attention(q, k, v)
            (x2,) = ffn(x2, layer=i, half=1, attn=(o.reshape(B * S, D), b_w_o[j]),
                        final_g=final_norm if i == depth - 1 else None)
    return x2.reshape(B, S, D)
```

```python
import functools
import math
from typing import NamedTuple

import jax
import jax.numpy as jnp
from jax import lax
from jax.experimental import pallas as pl
from jax.experimental.pallas import tpu as pltpu

D_MODEL = 1024
D_FF = 2816
D_A = 2 * D_MODEL
CHUNK = 128
SGU_GROUPS = 8
SGU_GROUP_DIM = D_A // SGU_GROUPS
SB_HEADS = 16
SB_HEAD_DIM = 64
SB_KV_HEADS = 4
SB_GROUP = SB_HEADS // SB_KV_HEADS
FFN_RES_SCALE = 0.5
NORM_EPS = 1e-6
LN_EPS = 1e-5
LOG2E = 1.4426950408889634
SOFTPLUS_CLAMP = 64.0
DEAD_RUN = -160.0
FINISHED_RUN = -1e30

VMEM_LIMIT_BYTES = 56 * 1024 * 1024

TOKEN_TILE = 512
FFN_CHUNK = 256
ATT_BLOCK = 256
ATT_Q_SUB = 4

BF16 = jnp.bfloat16
F32 = jnp.float32


def _const_spec(shape):
    nd = len(shape)
    return pl.BlockSpec(shape, lambda *_: (0,) * nd, pipeline_mode=pl.Buffered(1))


def _rms_norm(x, g):
    return x * lax.rsqrt(jnp.mean(x * x, axis=-1, keepdims=True) + NORM_EPS) * g


def _gelu_tanh(x):
    c1 = math.sqrt(2.0 / math.pi)
    c2 = c1 * 0.044715
    return x * (0.5 + 0.5 * jnp.tanh(x * (c1 + c2 * (x * x))))


def _params(n_grid_dims):
    return pltpu.CompilerParams(
        dimension_semantics=("parallel",) * n_grid_dims,
        vmem_limit_bytes=VMEM_LIMIT_BYTES)


class _Cast(NamedTuple):
    src: jax.Array
    lead: tuple
    rows: int
    cols: int


def _cast_io(casts, steps):
    in_specs, out_specs, out_shape, args = [], [], [], []
    for c in casts:
        slab = c.rows // steps
        lead = c.lead
        in_specs.append(pl.BlockSpec((None,) * len(lead) + (slab, c.cols),
                                     lambda i, lead=lead: lead + (i, 0)))
        out_specs.append(pl.BlockSpec((slab, c.cols), lambda i: (i, 0)))
        out_shape.append(jax.ShapeDtypeStruct((c.rows, c.cols), BF16))
        args.append(c.src)
    return in_specs, out_specs, out_shape, args


def _do_casts(cast_in_refs, cast_out_refs):
    for src, dst in zip(cast_in_refs, cast_out_refs):
        dst[...] = src[...].astype(BF16)


class _FfnPlan(NamedTuple):
    attn_in: bool
    final_norm: bool
    proj_outs: int
    proj_scale: float
    n_casts: int


def _ffn_kernel(*refs, plan):
    it = iter(refs)
    x_ref = next(it)
    if plan.attn_in:
        o_ref, wo_ref = next(it), next(it)
    g_ref, win_ref, wout_ref = next(it), next(it), next(it)
    if plan.final_norm:
        fg_ref = next(it)
    if plan.proj_outs:
        pg_ref, pw_ref = next(it), next(it)
    cast_in_refs = [next(it) for _ in range(plan.n_casts)]
    y_ref = next(it)
    proj_refs = [next(it) for _ in range(plan.proj_outs)]
    _do_casts(cast_in_refs, list(it))

    x = x_ref[...]
    if plan.attn_in:
        x = x + jnp.dot(o_ref[...], wo_ref[...], preferred_element_type=F32)
    h = _rms_norm(x, g_ref[...]).astype(BF16)
    acc = jnp.zeros(x.shape, F32)
    for c in range(D_FF // FFN_CHUNK):
        lo = c * FFN_CHUNK
        gate = jnp.dot(h, win_ref[:, lo:lo + FFN_CHUNK], preferred_element_type=F32)
        up = jnp.dot(h, win_ref[:, D_FF + lo:D_FF + lo + FFN_CHUNK], preferred_element_type=F32)
        act = (gate * jax.nn.sigmoid(gate) * up).astype(BF16)
        acc = acc + jnp.dot(act, wout_ref[lo:lo + FFN_CHUNK, :], preferred_element_type=F32)
    y = x + FFN_RES_SCALE * acc
    if plan.proj_outs:
        p = jnp.dot(_rms_norm(y, pg_ref[...]).astype(BF16), pw_ref[...],
                    preferred_element_type=F32)
        if plan.proj_scale != 1.0:
            p = p * plan.proj_scale
        p = p.astype(BF16)
        heads = proj_refs[0].shape[1]
        for oi, p_ref in enumerate(proj_refs):
            for hd in range(heads):
                c0 = (oi * heads + hd) * SB_HEAD_DIM
                p_ref[0, hd] = p[:, c0:c0 + SB_HEAD_DIM]
    if plan.final_norm:
        y = _rms_norm(y, fg_ref[...])
    y_ref[...] = y


def _ffn(x2, seq_len, gain, w_in, w_out, name, *, attn=None, final_g=None, proj=None, casts=()):
    T = x2.shape[0]
    tm = TOKEN_TILE
    plan = _FfnPlan(attn is not None, final_g is not None,
                    0 if proj is None else proj[2], 1.0 if proj is None else proj[3], len(casts))
    row = pl.BlockSpec((tm, D_MODEL), lambda i: (i, 0))

    in_specs, args = [row], [x2]
    if plan.attn_in:
        o2, w_o = attn
        in_specs += [row, _const_spec(w_o.shape)]
        args += [o2, w_o]
    in_specs += [_const_spec((1, D_MODEL)), _const_spec(w_in.shape), _const_spec(w_out.shape)]
    args += [gain.reshape(1, D_MODEL), w_in, w_out]
    if plan.final_norm:
        in_specs.append(_const_spec((1, D_MODEL)))
        args.append(final_g.reshape(1, D_MODEL))
    out_specs, out_shape = [row], [jax.ShapeDtypeStruct((T, D_MODEL), F32)]
    if plan.proj_outs:
        p_g, p_w = proj[0], proj[1]
        in_specs += [_const_spec((1, D_MODEL)), _const_spec(p_w.shape)]
        args += [p_g.reshape(1, D_MODEL), p_w]
        heads = p_w.shape[1] // (plan.proj_outs * SB_HEAD_DIM)
        tiles_per_seq = seq_len // tm
        out_specs += [pl.BlockSpec((1, heads, tm, SB_HEAD_DIM),
                                   lambda i: (i // tiles_per_seq, 0, i % tiles_per_seq, 0))
                      ] * plan.proj_outs
        out_shape += [jax.ShapeDtypeStruct((T // seq_len, heads, seq_len, SB_HEAD_DIM), BF16)
                      ] * plan.proj_outs
    c_in, c_out, c_shape, c_args = _cast_io(casts, T // tm)
    return pl.pallas_call(
        functools.partial(_ffn_kernel, plan=plan),
        grid=(T // tm,),
        in_specs=in_specs + c_in,
        out_specs=out_specs + c_out,
        out_shape=out_shape + c_shape,
        compiler_params=_params(1),
        name=name,
    )(*args, *c_args)


def _sgu_kernel(x_ref, g_ref, win_ref, lng_ref, lnb_ref, ws_ref, bs_ref, wout_ref, *rest):
    n_casts = (len(rest) - 1) // 2
    o_ref = rest[n_casts]
    _do_casts(rest[:n_casts], rest[n_casts + 1:])

    x = x_ref[...]
    tm = x.shape[0]
    h = _rms_norm(x, g_ref[...]).astype(BF16)
    def in_proj(c0):
        return jnp.dot(h, win_ref[:, c0:c0 + SGU_GROUP_DIM], preferred_element_type=F32)

    v = jnp.concatenate([_gelu_tanh(in_proj(D_A + gi * SGU_GROUP_DIM))
                         for gi in range(SGU_GROUPS)], axis=1)
    u_raw = [in_proj(gi * SGU_GROUP_DIM) for gi in range(SGU_GROUPS)]
    mu = jnp.mean(v, axis=-1, keepdims=True)
    vc = v - mu
    var = jnp.mean(vc * vc, axis=-1, keepdims=True)
    vn = (vc * lax.rsqrt(var + LN_EPS) * lng_ref[...] + lnb_ref[...]).astype(BF16)

    row = lax.broadcasted_iota(jnp.int32, (CHUNK, CHUNK), 0)
    col = lax.broadcasted_iota(jnp.int32, (CHUNK, CHUNK), 1)
    causal = row >= col
    bs = bs_ref[...]

    def spatial(gi):
        ws = jnp.where(causal, ws_ref[gi], 0.0).astype(BF16)
        bias = bs[:, gi:gi + 1]
        c0 = gi * SGU_GROUP_DIM
        return jnp.concatenate(
            [jnp.dot(ws, vn[n * CHUNK:(n + 1) * CHUNK, c0:c0 + SGU_GROUP_DIM],
                     preferred_element_type=F32) + bias for n in range(tm // CHUNK)], axis=0)

    acc = jnp.zeros(x.shape, F32)
    vs_next = spatial(0)
    for gi in range(SGU_GROUPS):
        vs = vs_next
        if gi + 1 < SGU_GROUPS:
            vs_next = spatial(gi + 1)
        gated = (_gelu_tanh(u_raw[gi]) * vs).astype(BF16)
        acc = acc + jnp.dot(gated, wout_ref[gi * SGU_GROUP_DIM:(gi + 1) * SGU_GROUP_DIM, :],
                            preferred_element_type=F32)
    o_ref[...] = x + acc


def _sgu(x2, g, w_in, ln_g, ln_b, w_s, b_s, w_out, casts=()):
    T = x2.shape[0]
    tm = TOKEN_TILE
    row = pl.BlockSpec((tm, D_MODEL), lambda i: (i, 0))
    c_in, c_out, c_shape, c_args = _cast_io(casts, T // tm)
    return pl.pallas_call(
        _sgu_kernel,
        grid=(T // tm,),
        in_specs=[row, _const_spec((1, D_MODEL)), _const_spec(w_in.shape),
                  _const_spec((1, D_A)), _const_spec((1, D_A)),
                  _const_spec(w_s.shape), _const_spec((CHUNK, SGU_GROUPS)),
                  _const_spec(w_out.shape)] + c_in,
        out_specs=[row] + c_out,
        out_shape=[jax.ShapeDtypeStruct((T, D_MODEL), F32)] + c_shape,
        compiler_params=_params(1),
        name="sgu",
    )(x2, g.reshape(1, D_MODEL), w_in, ln_g.reshape(1, D_A), ln_b.reshape(1, D_A),
      w_s, b_s.T, w_out, *c_args)


def _attn_kernel(q_ref, k_ref, v_ref, o_ref, acc_ref, run_ref):
    qt = pl.program_id(2)
    blk = ATT_BLOCK
    nsub = ATT_Q_SUB
    row = lax.broadcasted_iota(jnp.int32, (blk, blk), 0)
    col = lax.broadcasted_iota(jnp.int32, (blk, blk), 1)
    suffix = jnp.where(row > col, -1.0, 0.0).astype(BF16)
    strictly_earlier = col < row

    def load_kv(j):
        rows = pl.ds(pl.multiple_of(jnp.maximum(j, 0) * blk, blk), blk)
        return k_ref[0, 0, rows, :], v_ref[0, 0, rows, :]

    def finished_after(j):
        return jnp.where(j >= 1, 0.0, FINISHED_RUN).astype(F32)

    def scores(unit):
        sub, hd, (kb, _), diag, retire = unit
        idx = sub * SB_GROUP + hd
        z = lax.dot_general(q_ref[0, hd, sub * blk:(sub + 1) * blk, :], kb,
                            (((1,), (1,)), ((), ())), preferred_element_type=F32)
        sp = jnp.maximum(z, jnp.log2(1.0 + jnp.exp2(jnp.minimum(z, SOFTPLUS_CLAMP))))
        if diag:
            sp = jnp.where(strictly_earlier, sp, 0.0)
        run = 0.0 if diag else run_ref[idx]
        new_run = run - jnp.broadcast_to(jnp.sum(sp, axis=-1, keepdims=True), (blk, 128))
        if retire is not None:
            new_run = new_run + retire
        run_ref[idx] = new_run
        w = z - sp
        if not diag:
            w = w + jnp.concatenate([run, run], axis=1)
        return sp.astype(BF16), w, new_run

    def weights(unit, st):
        sp, w = st
        a = jnp.exp2(w + jnp.dot(sp, suffix, preferred_element_type=F32))
        if unit[3]:
            a = jnp.where(strictly_earlier, a, 0.0)
        return a.astype(BF16)

    def accumulate(unit, a):
        sub, hd, (_, vb), diag, _ = unit
        idx = sub * SB_GROUP + hd
        pv = jnp.dot(a, vb, preferred_element_type=F32)
        if diag:
            acc_ref[idx] = pv
        else:
            acc_ref[idx] += pv

    def run_units(units, tracked):
        n = len(units)
        st, a = {}, {}
        max_run = None
        for step in range(n + 2):
            if step < n:
                sp, w, new_run = scores(units[step])
                st[step] = (sp, w)
                if step in tracked:
                    max_run = new_run if max_run is None else jnp.maximum(max_run, new_run)
            if 0 <= step - 1 < n:
                a[step - 1] = weights(units[step - 1], st.pop(step - 1))
            if step - 2 >= 0:
                accumulate(units[step - 2], a.pop(step - 2))
        return max_run

    def alive(max_run):
        return (jnp.max(max_run) > DEAD_RUN).astype(jnp.int32)

    first = qt * nsub
    units, tracked = [], set()
    for off in range(nsub - 1, -2, -1):
        j = first + off
        kv = load_kv(j)
        retire = finished_after(j) if off <= 0 else None
        for sub in (off, off + 1):
            if 0 <= sub < nsub:
                for hd in range(SB_GROUP):
                    if sub == off + 1:
                        tracked.add(len(units))
                    units.append((sub, hd, kv, sub == off, retire))
    head_max = run_units(units, tracked)

    n_rounds = first + nsub - 2

    def more_rounds(carry):
        i, live = carry
        return jnp.logical_and(i < n_rounds, live > 0)

    def one_round(carry):
        i, _ = carry
        units = []
        for sub in range(nsub):
            j = first + sub - 2 - i
            kv = load_kv(j)
            retire = finished_after(j)
            units += [(sub, hd, kv, False, retire) for hd in range(SB_GROUP)]
        return i + 1, alive(run_units(units, set(range(len(units)))))

    lax.while_loop(more_rounds, one_round, (jnp.int32(0), alive(head_max)))

    for sub in range(nsub):
        o_ref[0, sub * blk:(sub + 1) * blk, :] = jnp.concatenate(
            [acc_ref[sub * SB_GROUP + hd] for hd in range(SB_GROUP)], axis=1).astype(o_ref.dtype)


def _attention(q, k, v):
    B, _, S, _ = q.shape
    tq = ATT_BLOCK * ATT_Q_SUB
    kv_spec = pl.BlockSpec((1, 1, S, SB_HEAD_DIM), lambda b, kh, i: (b, kh, 0, 0))
    return pl.pallas_call(
        _attn_kernel,
        grid=(B, SB_KV_HEADS, S // tq),
        in_specs=[pl.BlockSpec((1, SB_GROUP, tq, SB_HEAD_DIM), lambda b, kh, i: (b, kh, i, 0)),
                  kv_spec, kv_spec],
        out_specs=pl.BlockSpec((1, tq, SB_GROUP * SB_HEAD_DIM), lambda b, kh, i: (b, i, kh)),
        out_shape=jax.ShapeDtypeStruct((B, S, SB_HEADS * SB_HEAD_DIM), BF16),
        scratch_shapes=[pltpu.VMEM((ATT_Q_SUB * SB_GROUP, ATT_BLOCK, SB_HEAD_DIM), F32),
                        pltpu.VMEM((ATT_Q_SUB * SB_GROUP, ATT_BLOCK, 128), F32)],
        compiler_params=_params(3),
        name="sb_attention",
    )(q, k, v)


def kernel(x, ffn_norm, ffn_w_in, ffn_w_out, mix_norm, a_w_in, a_ln_g, a_ln_b, a_w_s, a_b_s,
           a_w_out, kv_norm, w_kv, b_w_q, b_w_o, final_norm):
    B, S, D = x.shape
    depth = ffn_norm.shape[0]
    n_a = a_w_in.shape[0]
    assert 1 <= n_a < depth, "shared K/V are produced by the last mixer-A layer"
    q_scale = LOG2E / math.sqrt(SB_HEAD_DIM)
    w_out_view = ffn_w_out.reshape(depth, 2, D, D_FF)

    def ffn_casts(layer, half):
        return [_Cast(ffn_w_in, (layer, half), D, 2 * D_FF),
                _Cast(w_out_view, (layer, half), D, D_FF)]

    def plain(w):
        return _Cast(w, (), w.shape[0], w.shape[1])

    def as_ffn_weights(cast_in, cast_out):
        return cast_in, cast_out.reshape(D_FF, D)

    w_in, w_out = ffn_w_in[0, 0].astype(BF16), ffn_w_out[0, 0].astype(BF16)
    x2 = x.reshape(B * S, D)
    k = v = None
    for i in range(depth):
        if i < n_a:
            x2, a_in, a_out = _ffn(x2, S, ffn_norm[i, 0], w_in, w_out, "ffn_l%d_h0" % i,
                                   casts=[plain(a_w_in[i]), plain(a_w_out[i])])
            outs = _sgu(x2, mix_norm[i], a_in, a_ln_g[i], a_ln_b[i], a_w_s[i], a_b_s[i], a_out,
                        casts=ffn_casts(i, 1) + ([plain(w_kv)] if i == n_a - 1 else []))
            x2 = outs[0]
            w_in, w_out = as_ffn_weights(outs[1], outs[2])
            if i == n_a - 1:
                x2, k, v, w_in, w_out, w_q = _ffn(
                    x2, S, ffn_norm[i, 1], w_in, w_out, "ffn_l%d_h1" % i,
                    proj=(kv_norm, outs[3], 2, 1.0),
                    casts=ffn_casts(i + 1, 0) + [plain(b_w_q[0])])
            else:
                x2, w_in, w_out = _ffn(x2, S, ffn_norm[i, 1], w_in, w_out, "ffn_l%d_h1" % i,
                                       casts=ffn_casts(i + 1, 0))
            w_in, w_out = as_ffn_weights(w_in, w_out)
        else:
            j = i - n_a
            x2, q, w_in, w_out, w_o = _ffn(
                x2, S, ffn_norm[i, 0], w_in, w_out, "ffn_l%d_h0" % i,
                proj=(mix_norm[i], w_q, 1, q_scale),
                casts=ffn_casts(i, 1) + [plain(b_w_o[j])])
            w_in, w_out = as_ffn_weights(w_in, w_out)
            o = _attention(q, k, v)
            last = i == depth - 1
            casts = [] if last else ffn_casts(i + 1, 0) + [plain(b_w_q[j + 1])]
            outs = _ffn(x2, S, ffn_norm[i, 1], w_in, w_out, "ffn_l%d_h1" % i,
                        attn=(o.reshape(B * S, D), w_o),
                        final_g=final_norm if last else None, casts=casts)
            x2 = outs[0]
            if not last:
                w_in, w_out = as_ffn_weights(outs[1], outs[2])
                w_q = outs[3]
    return x2.reshape(B, S, D)
```

```python
import functools
import math
from typing import NamedTuple

import jax
import jax.numpy as jnp
from jax import lax
from jax.experimental import pallas as pl
from jax.experimental.pallas import tpu as pltpu

D_MODEL = 1024
D_FF = 2816
D_A = 2 * D_MODEL
CHUNK = 128
SGU_GROUPS = 8
SGU_GROUP_DIM = D_A // SGU_GROUPS
SB_HEADS = 16
SB_HEAD_DIM = 64
SB_KV_HEADS = 4
SB_GROUP = SB_HEADS // SB_KV_HEADS
FFN_RES_SCALE = 0.5
NORM_EPS = 1e-6
LN_EPS = 1e-5
LOG2E = 1.4426950408889634
SOFTPLUS_CLAMP = 64.0
DEAD_RUN = -160.0
FINISHED_RUN = -1e30

VMEM_LIMIT_BYTES = 56 * 1024 * 1024

BF16_SUBLANE_TILE = 16

TOKEN_TILE = 512
FFN_CHUNK = 256
ATT_BLOCK = 256
ATT_Q_SUB = 4

BF16 = jnp.bfloat16
F32 = jnp.float32


def _const_spec(shape):
    nd = len(shape)
    return pl.BlockSpec(shape, lambda *_: (0,) * nd, pipeline_mode=pl.Buffered(1))


def _rms_norm(x, g):
    return x * lax.rsqrt(jnp.mean(x * x, axis=-1, keepdims=True) + NORM_EPS) * g


def _gelu_tanh(x):
    c1 = math.sqrt(2.0 / math.pi)
    c2 = c1 * 0.044715
    return x * (0.5 + 0.5 * jnp.tanh(x * (c1 + c2 * (x * x))))


def _params(n_grid_dims, semantics="parallel"):
    return pltpu.CompilerParams(
        dimension_semantics=(semantics,) * n_grid_dims,
        vmem_limit_bytes=VMEM_LIMIT_BYTES)


class _Cast(NamedTuple):
    src: jax.Array
    lead: tuple
    rows: int
    cols: int


def _cast_io(casts, steps):
    in_specs, out_specs, out_shape, args = [], [], [], []
    for c in casts:
        n_slabs = max(n for n in range(1, steps + 1)
                      if c.rows % n == 0 and (c.rows // n) % BF16_SUBLANE_TILE == 0)
        slab = c.rows // n_slabs
        lead = c.lead
        in_specs.append(pl.BlockSpec(
            (None,) * len(lead) + (slab, c.cols),
            lambda i, lead=lead, last=n_slabs - 1: lead + (jnp.minimum(i, last), 0)))
        out_specs.append(pl.BlockSpec(
            (slab, c.cols), lambda i, last=n_slabs - 1: (jnp.minimum(i, last), 0)))
        out_shape.append(jax.ShapeDtypeStruct((c.rows, c.cols), BF16))
        args.append(c.src)
    return in_specs, out_specs, out_shape, args


def _do_casts(cast_in_refs, cast_out_refs):
    for src, dst in zip(cast_in_refs, cast_out_refs):
        dst[...] = src[...].astype(BF16)


class _FfnPlan(NamedTuple):
    attn_in: bool
    final_norm: bool
    proj_outs: int
    proj_scale: float
    n_casts: int


def _ffn_kernel(*refs, plan):
    it = iter(refs)
    x_ref = next(it)
    if plan.attn_in:
        o_ref, wo_ref = next(it), next(it)
    g_ref, win_ref, wout_ref = next(it), next(it), next(it)
    if plan.final_norm:
        fg_ref = next(it)
    if plan.proj_outs:
        pg_ref, pw_ref = next(it), next(it)
    cast_in_refs = [next(it) for _ in range(plan.n_casts)]
    y_ref = next(it)
    proj_refs = [next(it) for _ in range(plan.proj_outs)]
    _do_casts(cast_in_refs, list(it))

    x = x_ref[...]
    if plan.attn_in:
        x = x + jnp.dot(o_ref[...], wo_ref[...], preferred_element_type=F32)
    h = _rms_norm(x, g_ref[...]).astype(BF16)
    acc = jnp.zeros(x.shape, F32)
    for c in range(D_FF // FFN_CHUNK):
        lo = c * FFN_CHUNK
        gate = jnp.dot(h, win_ref[:, lo:lo + FFN_CHUNK], preferred_element_type=F32)
        up = jnp.dot(h, win_ref[:, D_FF + lo:D_FF + lo + FFN_CHUNK], preferred_element_type=F32)
        act = (gate * jax.nn.sigmoid(gate) * up).astype(BF16)
        acc = acc + jnp.dot(act, wout_ref[lo:lo + FFN_CHUNK, :], preferred_element_type=F32)
    y = x + FFN_RES_SCALE * acc
    if plan.proj_outs:
        p = jnp.dot(_rms_norm(y, pg_ref[...]).astype(BF16), pw_ref[...],
                    preferred_element_type=F32)
        if plan.proj_scale != 1.0:
            p = p * plan.proj_scale
        p = p.astype(BF16)
        heads = proj_refs[0].shape[1]
        for oi, p_ref in enumerate(proj_refs):
            for hd in range(heads):
                c0 = (oi * heads + hd) * SB_HEAD_DIM
                p_ref[0, hd] = p[:, c0:c0 + SB_HEAD_DIM]
    if plan.final_norm:
        y = _rms_norm(y, fg_ref[...])
    y_ref[...] = y


def _ffn(x2, seq_len, gain, w_in, w_out, name, *, attn=None, final_g=None, proj=None, casts=()):
    T = x2.shape[0]
    tm = TOKEN_TILE
    plan = _FfnPlan(attn is not None, final_g is not None,
                    0 if proj is None else proj[2], 1.0 if proj is None else proj[3], len(casts))
    row = pl.BlockSpec((tm, D_MODEL), lambda i: (i, 0))

    in_specs, args = [row], [x2]
    if plan.attn_in:
        o2, w_o = attn
        in_specs += [row, _const_spec(w_o.shape)]
        args += [o2, w_o]
    in_specs += [_const_spec((1, D_MODEL)), _const_spec(w_in.shape), _const_spec(w_out.shape)]
    args += [gain.reshape(1, D_MODEL), w_in, w_out]
    if plan.final_norm:
        in_specs.append(_const_spec((1, D_MODEL)))
        args.append(final_g.reshape(1, D_MODEL))
    out_specs, out_shape = [row], [jax.ShapeDtypeStruct((T, D_MODEL), F32)]
    if plan.proj_outs:
        p_g, p_w = proj[0], proj[1]
        in_specs += [_const_spec((1, D_MODEL)), _const_spec(p_w.shape)]
        args += [p_g.reshape(1, D_MODEL), p_w]
        heads = p_w.shape[1] // (plan.proj_outs * SB_HEAD_DIM)
        tiles_per_seq = seq_len // tm
        out_specs += [pl.BlockSpec((1, heads, tm, SB_HEAD_DIM),
                                   lambda i: (i // tiles_per_seq, 0, i % tiles_per_seq, 0))
                      ] * plan.proj_outs
        out_shape += [jax.ShapeDtypeStruct((T // seq_len, heads, seq_len, SB_HEAD_DIM), BF16)
                      ] * plan.proj_outs
    c_in, c_out, c_shape, c_args = _cast_io(casts, T // tm)
    return pl.pallas_call(
        functools.partial(_ffn_kernel, plan=plan),
        grid=(T // tm,),
        in_specs=in_specs + c_in,
        out_specs=out_specs + c_out,
        out_shape=out_shape + c_shape,
        compiler_params=_params(1, "arbitrary"),
        name=name,
    )(*args, *c_args)


def _sgu_kernel(x_ref, g_ref, win_ref, lng_ref, lnb_ref, ws_ref, bs_ref, wout_ref, *rest):
    n_casts = (len(rest) - 1) // 2
    o_ref = rest[n_casts]
    _do_casts(rest[:n_casts], rest[n_casts + 1:])

    x = x_ref[...]
    tm = x.shape[0]
    h = _rms_norm(x, g_ref[...]).astype(BF16)
    def in_proj(c0):
        return jnp.dot(h, win_ref[:, c0:c0 + SGU_GROUP_DIM], preferred_element_type=F32)

    v = jnp.concatenate([_gelu_tanh(in_proj(D_A + gi * SGU_GROUP_DIM))
                         for gi in range(SGU_GROUPS)], axis=1)
    u_raw = [in_proj(gi * SGU_GROUP_DIM) for gi in range(SGU_GROUPS)]
    mu = jnp.mean(v, axis=-1, keepdims=True)
    vc = v - mu
    var = jnp.mean(vc * vc, axis=-1, keepdims=True)
    vn = (vc * lax.rsqrt(var + LN_EPS) * lng_ref[...] + lnb_ref[...]).astype(BF16)

    row = lax.broadcasted_iota(jnp.int32, (CHUNK, CHUNK), 0)
    col = lax.broadcasted_iota(jnp.int32, (CHUNK, CHUNK), 1)
    causal = row >= col
    bs = bs_ref[...]

    def spatial(gi):
        ws = jnp.where(causal, ws_ref[gi], 0.0).astype(BF16)
        bias = bs[:, gi:gi + 1]
        c0 = gi * SGU_GROUP_DIM
        return jnp.concatenate(
            [jnp.dot(ws, vn[n * CHUNK:(n + 1) * CHUNK, c0:c0 + SGU_GROUP_DIM],
                     preferred_element_type=F32) + bias for n in range(tm // CHUNK)], axis=0)

    acc = jnp.zeros(x.shape, F32)
    vs_next = spatial(0)
    for gi in range(SGU_GROUPS):
        vs = vs_next
        if gi + 1 < SGU_GROUPS:
            vs_next = spatial(gi + 1)
        gated = (_gelu_tanh(u_raw[gi]) * vs).astype(BF16)
        acc = acc + jnp.dot(gated, wout_ref[gi * SGU_GROUP_DIM:(gi + 1) * SGU_GROUP_DIM, :],
                            preferred_element_type=F32)
    o_ref[...] = x + acc


def _sgu(x2, g, w_in, ln_g, ln_b, w_s, b_s, w_out, casts=()):
    T = x2.shape[0]
    tm = TOKEN_TILE
    row = pl.BlockSpec((tm, D_MODEL), lambda i: (i, 0))
    c_in, c_out, c_shape, c_args = _cast_io(casts, T // tm)
    return pl.pallas_call(
        _sgu_kernel,
        grid=(T // tm,),
        in_specs=[row, _const_spec((1, D_MODEL)), _const_spec(w_in.shape),
                  _const_spec((1, D_A)), _const_spec((1, D_A)),
                  _const_spec(w_s.shape), _const_spec((CHUNK, SGU_GROUPS)),
                  _const_spec(w_out.shape)] + c_in,
        out_specs=[row] + c_out,
        out_shape=[jax.ShapeDtypeStruct((T, D_MODEL), F32)] + c_shape,
        compiler_params=_params(1, "arbitrary"),
        name="sgu",
    )(x2, g.reshape(1, D_MODEL), w_in, ln_g.reshape(1, D_A), ln_b.reshape(1, D_A),
      w_s, b_s.T, w_out, *c_args)


def _attn_kernel(q_ref, k_ref, v_ref, o_ref, acc_ref, run_ref):
    qt = pl.program_id(2)
    blk = ATT_BLOCK
    nsub = ATT_Q_SUB
    row = lax.broadcasted_iota(jnp.int32, (blk, blk), 0)
    col = lax.broadcasted_iota(jnp.int32, (blk, blk), 1)
    suffix = jnp.where(row > col, -1.0, 0.0).astype(BF16)
    strictly_earlier = col < row

    def load_kv(j):
        rows = pl.ds(pl.multiple_of(jnp.maximum(j, 0) * blk, blk), blk)
        return k_ref[0, 0, rows, :], v_ref[0, 0, rows, :]

    def finished_after(j):
        return jnp.where(j >= 1, 0.0, FINISHED_RUN).astype(F32)

    def scores(unit):
        sub, hd, (kb, _), diag, retire = unit
        idx = sub * SB_GROUP + hd
        z = lax.dot_general(q_ref[0, hd, sub * blk:(sub + 1) * blk, :], kb,
                            (((1,), (1,)), ((), ())), preferred_element_type=F32)
        sp = jnp.maximum(z, jnp.log2(1.0 + jnp.exp2(jnp.minimum(z, SOFTPLUS_CLAMP))))
        if diag:
            sp = jnp.where(strictly_earlier, sp, 0.0)
        run = 0.0 if diag else run_ref[idx]
        new_run = run - jnp.broadcast_to(jnp.sum(sp, axis=-1, keepdims=True), (blk, 128))
        if retire is not None:
            new_run = new_run + retire
        run_ref[idx] = new_run
        w = z - sp
        if not diag:
            w = w + jnp.concatenate([run, run], axis=1)
        return sp.astype(BF16), w, new_run

    def weights(unit, st):
        sp, w = st
        a = jnp.exp2(w + jnp.dot(sp, suffix, preferred_element_type=F32))
        if unit[3]:
            a = jnp.where(strictly_earlier, a, 0.0)
        return a.astype(BF16)

    def accumulate(unit, a):
        sub, hd, (_, vb), diag, _ = unit
        idx = sub * SB_GROUP + hd
        pv = jnp.dot(a, vb, preferred_element_type=F32)
        if diag:
            acc_ref[idx] = pv
        else:
            acc_ref[idx] += pv

    def run_units(units, tracked):
        n = len(units)
        st, a = {}, {}
        max_run = None
        for step in range(n + 2):
            if step < n:
                sp, w, new_run = scores(units[step])
                st[step] = (sp, w)
                if step in tracked:
                    max_run = new_run if max_run is None else jnp.maximum(max_run, new_run)
            if 0 <= step - 1 < n:
                a[step - 1] = weights(units[step - 1], st.pop(step - 1))
            if step - 2 >= 0:
                accumulate(units[step - 2], a.pop(step - 2))
        return max_run

    def alive(max_run):
        return (jnp.max(max_run) > DEAD_RUN).astype(jnp.int32)

    first = qt * nsub
    units, tracked = [], set()
    for off in range(nsub - 1, -2, -1):
        j = first + off
        kv = load_kv(j)
        retire = finished_after(j) if off <= 0 else None
        for sub in (off, off + 1):
            if 0 <= sub < nsub:
                for hd in range(SB_GROUP):
                    if sub == off + 1:
                        tracked.add(len(units))
                    units.append((sub, hd, kv, sub == off, retire))
    head_max = run_units(units, tracked)

    n_rounds = first + nsub - 2

    def more_rounds(carry):
        i, live = carry
        return jnp.logical_and(i < n_rounds, live > 0)

    def one_round(carry):
        i, _ = carry
        units = []
        for sub in range(nsub):
            j = first + sub - 2 - i
            kv = load_kv(j)
            retire = finished_after(j)
            units += [(sub, hd, kv, False, retire) for hd in range(SB_GROUP)]
        return i + 1, alive(run_units(units, set(range(len(units)))))

    lax.while_loop(more_rounds, one_round, (jnp.int32(0), alive(head_max)))

    for sub in range(nsub):
        o_ref[0, sub * blk:(sub + 1) * blk, :] = jnp.concatenate(
            [acc_ref[sub * SB_GROUP + hd] for hd in range(SB_GROUP)], axis=1).astype(o_ref.dtype)


def _attention(q, k, v):
    B, _, S, _ = q.shape
    tq = ATT_BLOCK * ATT_Q_SUB
    kv_spec = pl.BlockSpec((1, 1, S, SB_HEAD_DIM), lambda b, kh, i: (b, kh, 0, 0))
    return pl.pallas_call(
        _attn_kernel,
        grid=(B, SB_KV_HEADS, S // tq),
        in_specs=[pl.BlockSpec((1, SB_GROUP, tq, SB_HEAD_DIM), lambda b, kh, i: (b, kh, i, 0)),
                  kv_spec, kv_spec],
        out_specs=pl.BlockSpec((1, tq, SB_GROUP * SB_HEAD_DIM), lambda b, kh, i: (b, i, kh)),
        out_shape=jax.ShapeDtypeStruct((B, S, SB_HEADS * SB_HEAD_DIM), BF16),
        scratch_shapes=[pltpu.VMEM((ATT_Q_SUB * SB_GROUP, ATT_BLOCK, SB_HEAD_DIM), F32),
                        pltpu.VMEM((ATT_Q_SUB * SB_GROUP, ATT_BLOCK, 128), F32)],
        compiler_params=_params(3),
        name="sb_attention",
    )(q, k, v)


def kernel(x, ffn_norm, ffn_w_in, ffn_w_out, mix_norm, a_w_in, a_ln_g, a_ln_b, a_w_s, a_b_s,
           a_w_out, kv_norm, w_kv, b_w_q, b_w_o, final_norm):
    B, S, D = x.shape
    depth = ffn_norm.shape[0]
    n_a = a_w_in.shape[0]
    assert 1 <= n_a < depth, "shared K/V are produced by the last mixer-A layer"
    q_scale = LOG2E / math.sqrt(SB_HEAD_DIM)

    def ffn_casts(layer, half):
        return [_Cast(ffn_w_in, (layer, half), D, 2 * D_FF),
                _Cast(ffn_w_out, (layer, half), D_FF, D)]

    def plain(w):
        return _Cast(w, (), w.shape[0], w.shape[1])

    w_in, w_out = ffn_w_in[0, 0].astype(BF16), ffn_w_out[0, 0].astype(BF16)
    x2 = x.reshape(B * S, D)
    k = v = None
    for i in range(depth):
        if i < n_a:
            x2, a_in, a_out = _ffn(x2, S, ffn_norm[i, 0], w_in, w_out, "ffn_l%d_h0" % i,
                                   casts=[plain(a_w_in[i]), plain(a_w_out[i])])
            outs = _sgu(x2, mix_norm[i], a_in, a_ln_g[i], a_ln_b[i], a_w_s[i], a_b_s[i], a_out,
                        casts=ffn_casts(i, 1) + ([plain(w_kv)] if i == n_a - 1 else []))
            x2, w_in, w_out = outs[:3]
            if i == n_a - 1:
                x2, k, v, w_in, w_out, w_q = _ffn(
                    x2, S, ffn_norm[i, 1], w_in, w_out, "ffn_l%d_h1" % i,
                    proj=(kv_norm, outs[3], 2, 1.0),
                    casts=ffn_casts(i + 1, 0) + [plain(b_w_q[0])])
            else:
                x2, w_in, w_out = _ffn(x2, S, ffn_norm[i, 1], w_in, w_out, "ffn_l%d_h1" % i,
                                       casts=ffn_casts(i + 1, 0))
        else:
            j = i - n_a
            x2, q, w_in, w_out, w_o = _ffn(
                x2, S, ffn_norm[i, 0], w_in, w_out, "ffn_l%d_h0" % i,
                proj=(mix_norm[i], w_q, 1, q_scale),
                casts=ffn_casts(i, 1) + [plain(b_w_o[j])])
            o = _attention(q, k, v)
            last = i == depth - 1
            casts = [] if last else ffn_casts(i + 1, 0) + [plain(b_w_q[j + 1])]
            outs = _ffn(x2, S, ffn_norm[i, 1], w_in, w_out, "ffn_l%d_h1" % i,
                        attn=(o.reshape(B * S, D), w_o),
                        final_g=final_norm if last else None, casts=casts)
            x2 = outs[0]
            if not last:
                w_in, w_out, w_q = outs[1:4]
    return x2.reshape(B, S, D)
```

```python
import functools
import math
from typing import NamedTuple

import jax
import jax.numpy as jnp
from jax import lax
from jax.experimental import pallas as pl
from jax.experimental.pallas import tpu as pltpu

D_MODEL = 1024
D_FF = 2816
D_A = 2 * D_MODEL
CHUNK = 128
SGU_GROUPS = 8
SGU_GROUP_DIM = D_A // SGU_GROUPS
SB_HEADS = 16
SB_HEAD_DIM = 64
SB_KV_HEADS = 4
SB_GROUP = SB_HEADS // SB_KV_HEADS
FFN_RES_SCALE = 0.5
NORM_EPS = 1e-6
LN_EPS = 1e-5
LOG2E = 1.4426950408889634
SOFTPLUS_CLAMP = 64.0
DEAD_RUN = -160.0
FINISHED_RUN = -1e30

VMEM_LIMIT_BYTES = 56 * 1024 * 1024

BF16_SUBLANE_TILE = 16

TOKEN_TILE = 512
FFN_TILE = 1024
FFN_CHUNK = 256
ATT_BLOCK = 256
ATT_Q_SUB = 8

BF16 = jnp.bfloat16
F32 = jnp.float32


def _const_spec(shape):
    nd = len(shape)
    return pl.BlockSpec(shape, lambda *_: (0,) * nd, pipeline_mode=pl.Buffered(1))


def _rms_norm(x, g):
    return x * lax.rsqrt(jnp.mean(x * x, axis=-1, keepdims=True) + NORM_EPS) * g


def _gelu_tanh(x):
    c1 = math.sqrt(2.0 / math.pi)
    c2 = c1 * 0.044715
    return x * (0.5 + 0.5 * jnp.tanh(x * (c1 + c2 * (x * x))))


def _params(n_grid_dims, semantics="parallel"):
    return pltpu.CompilerParams(
        dimension_semantics=(semantics,) * n_grid_dims,
        vmem_limit_bytes=VMEM_LIMIT_BYTES)


class _Cast(NamedTuple):
    src: jax.Array
    lead: tuple
    rows: int
    cols: int


def _cast_io(casts, steps):
    in_specs, out_specs, out_shape, args = [], [], [], []
    for c in casts:
        n_slabs = max(n for n in range(1, steps + 1)
                      if c.rows % n == 0 and (c.rows // n) % BF16_SUBLANE_TILE == 0)
        slab = c.rows // n_slabs
        lead = c.lead
        in_specs.append(pl.BlockSpec(
            (None,) * len(lead) + (slab, c.cols),
            lambda i, lead=lead, last=n_slabs - 1: lead + (jnp.minimum(i, last), 0)))
        out_specs.append(pl.BlockSpec(
            (slab, c.cols), lambda i, last=n_slabs - 1: (jnp.minimum(i, last), 0)))
        out_shape.append(jax.ShapeDtypeStruct((c.rows, c.cols), BF16))
        args.append(c.src)
    return in_specs, out_specs, out_shape, args


def _do_casts(cast_in_refs, cast_out_refs):
    for src, dst in zip(cast_in_refs, cast_out_refs):
        dst[...] = src[...].astype(BF16)


class _FfnPlan(NamedTuple):
    attn_in: bool
    final_norm: bool
    proj_outs: int
    proj_scale: float
    n_casts: int


def _ffn_kernel(*refs, plan):
    it = iter(refs)
    x_ref = next(it)
    if plan.attn_in:
        o_ref, wo_ref = next(it), next(it)
    g_ref, win_ref, wout_ref = next(it), next(it), next(it)
    if plan.final_norm:
        fg_ref = next(it)
    if plan.proj_outs:
        pg_ref, pw_ref = next(it), next(it)
    cast_in_refs = [next(it) for _ in range(plan.n_casts)]
    y_ref = next(it)
    proj_refs = [next(it) for _ in range(plan.proj_outs)]
    _do_casts(cast_in_refs, list(it))

    x = x_ref[...]
    if plan.attn_in:
        x = x + jnp.dot(o_ref[...], wo_ref[...], preferred_element_type=F32)
    h = _rms_norm(x, g_ref[...]).astype(BF16)
    acc = jnp.zeros(x.shape, F32)
    for c in range(D_FF // FFN_CHUNK):
        lo = c * FFN_CHUNK
        gate = jnp.dot(h, win_ref[:, lo:lo + FFN_CHUNK], preferred_element_type=F32)
        up = jnp.dot(h, win_ref[:, D_FF + lo:D_FF + lo + FFN_CHUNK], preferred_element_type=F32)
        act = (gate * jax.nn.sigmoid(gate) * up).astype(BF16)
        acc = acc + jnp.dot(act, wout_ref[lo:lo + FFN_CHUNK, :], preferred_element_type=F32)
    y = x + FFN_RES_SCALE * acc
    if plan.proj_outs:
        p = jnp.dot(_rms_norm(y, pg_ref[...]).astype(BF16), pw_ref[...],
                    preferred_element_type=F32)
        if plan.proj_scale != 1.0:
            p = p * plan.proj_scale
        p = p.astype(BF16)
        heads = proj_refs[0].shape[1]
        for oi, p_ref in enumerate(proj_refs):
            for hd in range(heads):
                c0 = (oi * heads + hd) * SB_HEAD_DIM
                p_ref[0, hd] = p[:, c0:c0 + SB_HEAD_DIM]
    if plan.final_norm:
        y = _rms_norm(y, fg_ref[...])
    y_ref[...] = y


def _ffn(x2, seq_len, gain, w_in, w_out, name, *, attn=None, final_g=None, proj=None, casts=()):
    T = x2.shape[0]
    tm = FFN_TILE
    plan = _FfnPlan(attn is not None, final_g is not None,
                    0 if proj is None else proj[2], 1.0 if proj is None else proj[3], len(casts))
    row = pl.BlockSpec((tm, D_MODEL), lambda i: (i, 0))

    in_specs, args = [row], [x2]
    if plan.attn_in:
        o2, w_o = attn
        in_specs += [row, _const_spec(w_o.shape)]
        args += [o2, w_o]
    in_specs += [_const_spec((1, D_MODEL)), _const_spec(w_in.shape), _const_spec(w_out.shape)]
    args += [gain.reshape(1, D_MODEL), w_in, w_out]
    if plan.final_norm:
        in_specs.append(_const_spec((1, D_MODEL)))
        args.append(final_g.reshape(1, D_MODEL))
    out_specs, out_shape = [row], [jax.ShapeDtypeStruct((T, D_MODEL), F32)]
    if plan.proj_outs:
        p_g, p_w = proj[0], proj[1]
        in_specs += [_const_spec((1, D_MODEL)), _const_spec(p_w.shape)]
        args += [p_g.reshape(1, D_MODEL), p_w]
        heads = p_w.shape[1] // (plan.proj_outs * SB_HEAD_DIM)
        tiles_per_seq = seq_len // tm
        out_specs += [pl.BlockSpec((1, heads, tm, SB_HEAD_DIM),
                                   lambda i: (i // tiles_per_seq, 0, i % tiles_per_seq, 0))
                      ] * plan.proj_outs
        out_shape += [jax.ShapeDtypeStruct((T // seq_len, heads, seq_len, SB_HEAD_DIM), BF16)
                      ] * plan.proj_outs
    c_in, c_out, c_shape, c_args = _cast_io(casts, T // tm)
    return pl.pallas_call(
        functools.partial(_ffn_kernel, plan=plan),
        grid=(T // tm,),
        in_specs=in_specs + c_in,
        out_specs=out_specs + c_out,
        out_shape=out_shape + c_shape,
        compiler_params=_params(1, "arbitrary"),
        name=name,
    )(*args, *c_args)


def _sgu_kernel(x_ref, g_ref, win_ref, lng_ref, lnb_ref, ws_ref, bs_ref, wout_ref, *rest):
    n_casts = (len(rest) - 1) // 2
    o_ref = rest[n_casts]
    _do_casts(rest[:n_casts], rest[n_casts + 1:])

    x = x_ref[...]
    tm = x.shape[0]
    h = _rms_norm(x, g_ref[...]).astype(BF16)
    def in_proj(c0):
        return jnp.dot(h, win_ref[:, c0:c0 + SGU_GROUP_DIM], preferred_element_type=F32)

    v = jnp.concatenate([_gelu_tanh(in_proj(D_A + gi * SGU_GROUP_DIM))
                         for gi in range(SGU_GROUPS)], axis=1)
    u_raw = [in_proj(gi * SGU_GROUP_DIM) for gi in range(SGU_GROUPS)]
    mu = jnp.mean(v, axis=-1, keepdims=True)
    vc = v - mu
    var = jnp.mean(vc * vc, axis=-1, keepdims=True)
    vn = (vc * lax.rsqrt(var + LN_EPS) * lng_ref[...] + lnb_ref[...]).astype(BF16)

    row = lax.broadcasted_iota(jnp.int32, (CHUNK, CHUNK), 0)
    col = lax.broadcasted_iota(jnp.int32, (CHUNK, CHUNK), 1)
    causal = row >= col
    bs = bs_ref[...]

    def spatial(gi):
        ws = jnp.where(causal, ws_ref[gi], 0.0).astype(BF16)
        bias = bs[:, gi:gi + 1]
        c0 = gi * SGU_GROUP_DIM
        return jnp.concatenate(
            [jnp.dot(ws, vn[n * CHUNK:(n + 1) * CHUNK, c0:c0 + SGU_GROUP_DIM],
                     preferred_element_type=F32) + bias for n in range(tm // CHUNK)], axis=0)

    acc = jnp.zeros(x.shape, F32)
    vs_next = spatial(0)
    for gi in range(SGU_GROUPS):
        vs = vs_next
        if gi + 1 < SGU_GROUPS:
            vs_next = spatial(gi + 1)
        gated = (_gelu_tanh(u_raw[gi]) * vs).astype(BF16)
        acc = acc + jnp.dot(gated, wout_ref[gi * SGU_GROUP_DIM:(gi + 1) * SGU_GROUP_DIM, :],
                            preferred_element_type=F32)
    o_ref[...] = x + acc


def _sgu(x2, g, w_in, ln_g, ln_b, w_s, b_s, w_out, casts=()):
    T = x2.shape[0]
    tm = TOKEN_TILE
    row = pl.BlockSpec((tm, D_MODEL), lambda i: (i, 0))
    c_in, c_out, c_shape, c_args = _cast_io(casts, T // tm)
    return pl.pallas_call(
        _sgu_kernel,
        grid=(T // tm,),
        in_specs=[row, _const_spec((1, D_MODEL)), _const_spec(w_in.shape),
                  _const_spec((1, D_A)), _const_spec((1, D_A)),
                  _const_spec(w_s.shape), _const_spec((CHUNK, SGU_GROUPS)),
                  _const_spec(w_out.shape)] + c_in,
        out_specs=[row] + c_out,
        out_shape=[jax.ShapeDtypeStruct((T, D_MODEL), F32)] + c_shape,
        compiler_params=_params(1, "arbitrary"),
        name="sgu",
    )(x2, g.reshape(1, D_MODEL), w_in, ln_g.reshape(1, D_A), ln_b.reshape(1, D_A),
      w_s, b_s.T, w_out, *c_args)


def _attn_kernel(q_ref, k_ref, v_ref, o_ref, acc_ref, run_ref):
    qt = pl.program_id(2)
    blk = ATT_BLOCK
    nsub = ATT_Q_SUB
    row = lax.broadcasted_iota(jnp.int32, (blk, blk), 0)
    col = lax.broadcasted_iota(jnp.int32, (blk, blk), 1)
    suffix = jnp.where(row > col, -1.0, 0.0).astype(BF16)
    strictly_earlier = col < row

    def load_kv(j):
        rows = pl.ds(pl.multiple_of(jnp.maximum(j, 0) * blk, blk), blk)
        return k_ref[0, 0, rows, :], v_ref[0, 0, rows, :]

    def finished_after(j):
        return jnp.where(j >= 1, 0.0, FINISHED_RUN).astype(F32)

    def scores(unit):
        sub, hd, (kb, _), diag, retire = unit
        idx = sub * SB_GROUP + hd
        z = lax.dot_general(q_ref[0, hd, sub * blk:(sub + 1) * blk, :], kb,
                            (((1,), (1,)), ((), ())), preferred_element_type=F32)
        sp = jnp.maximum(z, jnp.log2(1.0 + jnp.exp2(jnp.minimum(z, SOFTPLUS_CLAMP))))
        if diag:
            sp = jnp.where(strictly_earlier, sp, 0.0)
        run = 0.0 if diag else run_ref[idx]
        new_run = run - jnp.broadcast_to(jnp.sum(sp, axis=-1, keepdims=True), (blk, 128))
        if retire is not None:
            new_run = new_run + retire
        run_ref[idx] = new_run
        w = z - sp
        if not diag:
            w = w + jnp.concatenate([run, run], axis=1)
        return sp.astype(BF16), w, new_run

    def weights(unit, st):
        sp, w = st
        a = jnp.exp2(w + jnp.dot(sp, suffix, preferred_element_type=F32))
        if unit[3]:
            a = jnp.where(strictly_earlier, a, 0.0)
        return a.astype(BF16)

    def accumulate(unit, a):
        sub, hd, (_, vb), diag, _ = unit
        idx = sub * SB_GROUP + hd
        pv = jnp.dot(a, vb, preferred_element_type=F32)
        if diag:
            acc_ref[idx] = pv
        else:
            acc_ref[idx] += pv

    def run_units(units, tracked):
        n = len(units)
        st, a = {}, {}
        max_run = None
        for step in range(n + 2):
            if step < n:
                sp, w, new_run = scores(units[step])
                st[step] = (sp, w)
                if step in tracked:
                    max_run = new_run if max_run is None else jnp.maximum(max_run, new_run)
            if 0 <= step - 1 < n:
                a[step - 1] = weights(units[step - 1], st.pop(step - 1))
            if step - 2 >= 0:
                accumulate(units[step - 2], a.pop(step - 2))
        return max_run

    def alive(max_run):
        return (jnp.max(max_run) > DEAD_RUN).astype(jnp.int32)

    first = qt * nsub
    units, tracked = [], set()
    for off in range(nsub - 1, -2, -1):
        j = first + off
        kv = load_kv(j)
        retire = finished_after(j) if off <= 0 else None
        for sub in (off, off + 1):
            if 0 <= sub < nsub:
                for hd in range(SB_GROUP):
                    if sub == off + 1:
                        tracked.add(len(units))
                    units.append((sub, hd, kv, sub == off, retire))
    head_max = run_units(units, tracked)

    n_rounds = first + nsub - 2

    def more_rounds(carry):
        i, live = carry
        return jnp.logical_and(i < n_rounds, live > 0)

    def one_round(carry):
        i, _ = carry
        units = []
        for sub in range(nsub):
            j = first + sub - 2 - i
            kv = load_kv(j)
            retire = finished_after(j)
            units += [(sub, hd, kv, False, retire) for hd in range(SB_GROUP)]
        return i + 1, alive(run_units(units, set(range(len(units)))))

    lax.while_loop(more_rounds, one_round, (jnp.int32(0), alive(head_max)))

    for sub in range(nsub):
        o_ref[0, sub * blk:(sub + 1) * blk, :] = jnp.concatenate(
            [acc_ref[sub * SB_GROUP + hd] for hd in range(SB_GROUP)], axis=1).astype(o_ref.dtype)


def _attention(q, k, v):
    B, _, S, _ = q.shape
    tq = ATT_BLOCK * ATT_Q_SUB
    kv_spec = pl.BlockSpec((1, 1, S, SB_HEAD_DIM), lambda b, kh, i: (b, kh, 0, 0))
    return pl.pallas_call(
        _attn_kernel,
        grid=(B, SB_KV_HEADS, S // tq),
        in_specs=[pl.BlockSpec((1, SB_GROUP, tq, SB_HEAD_DIM), lambda b, kh, i: (b, kh, i, 0)),
                  kv_spec, kv_spec],
        out_specs=pl.BlockSpec((1, tq, SB_GROUP * SB_HEAD_DIM), lambda b, kh, i: (b, i, kh)),
        out_shape=jax.ShapeDtypeStruct((B, S, SB_HEADS * SB_HEAD_DIM), BF16),
        scratch_shapes=[pltpu.VMEM((ATT_Q_SUB * SB_GROUP, ATT_BLOCK, SB_HEAD_DIM), F32),
                        pltpu.VMEM((ATT_Q_SUB * SB_GROUP, ATT_BLOCK, 128), F32)],
        compiler_params=_params(3),
        name="sb_attention",
    )(q, k, v)


def kernel(x, ffn_norm, ffn_w_in, ffn_w_out, mix_norm, a_w_in, a_ln_g, a_ln_b, a_w_s, a_b_s,
           a_w_out, kv_norm, w_kv, b_w_q, b_w_o, final_norm):
    B, S, D = x.shape
    depth = ffn_norm.shape[0]
    n_a = a_w_in.shape[0]
    assert 1 <= n_a < depth, "shared K/V are produced by the last mixer-A layer"
    q_scale = LOG2E / math.sqrt(SB_HEAD_DIM)

    def ffn_casts(layer, half):
        return [_Cast(ffn_w_in, (layer, half), D, 2 * D_FF),
                _Cast(ffn_w_out, (layer, half), D_FF, D)]

    def plain(w):
        return _Cast(w, (), w.shape[0], w.shape[1])

    w_in, w_out = ffn_w_in[0, 0].astype(BF16), ffn_w_out[0, 0].astype(BF16)
    x2 = x.reshape(B * S, D)
    k = v = None
    for i in range(depth):
        if i < n_a:
            x2, a_in, a_out = _ffn(x2, S, ffn_norm[i, 0], w_in, w_out, "ffn_l%d_h0" % i,
                                   casts=[plain(a_w_in[i]), plain(a_w_out[i])])
            outs = _sgu(x2, mix_norm[i], a_in, a_ln_g[i], a_ln_b[i], a_w_s[i], a_b_s[i], a_out,
                        casts=ffn_casts(i, 1) + ([plain(w_kv)] if i == n_a - 1 else []))
            x2, w_in, w_out = outs[:3]
            if i == n_a - 1:
                x2, k, v, w_in, w_out, w_q = _ffn(
                    x2, S, ffn_norm[i, 1], w_in, w_out, "ffn_l%d_h1" % i,
                    proj=(kv_norm, outs[3], 2, 1.0),
                    casts=ffn_casts(i + 1, 0) + [plain(b_w_q[0])])
            else:
                x2, w_in, w_out = _ffn(x2, S, ffn_norm[i, 1], w_in, w_out, "ffn_l%d_h1" % i,
                                       casts=ffn_casts(i + 1, 0))
        else:
            j = i - n_a
            x2, q, w_in, w_out, w_o = _ffn(
                x2, S, ffn_norm[i, 0], w_in, w_out, "ffn_l%d_h0" % i,
                proj=(mix_norm[i], w_q, 1, q_scale),
                casts=ffn_casts(i, 1) + [plain(b_w_o[j])])
            o = _attention(q, k, v)
            last = i == depth - 1
            casts = [] if last else ffn_casts(i + 1, 0) + [plain(b_w_q[j + 1])]
            outs = _ffn(x2, S, ffn_norm[i, 1], w_in, w_out, "ffn_l%d_h1" % i,
                        attn=(o.reshape(B * S, D), w_o),
                        final_g=final_norm if last else None, casts=casts)
            x2 = outs[0]
            if not last:
                w_in, w_out, w_q = outs[1:4]
    return x2.reshape(B, S, D)
```

```python
import functools
import math
from typing import NamedTuple

import jax
import jax.numpy as jnp
from jax import lax
from jax.experimental import pallas as pl
from jax.experimental.pallas import tpu as pltpu

D_MODEL = 1024
D_FF = 2816
D_A = 2 * D_MODEL
CHUNK = 128
SGU_GROUPS = 8
SGU_GROUP_DIM = D_A // SGU_GROUPS
SB_HEADS = 16
SB_HEAD_DIM = 64
SB_KV_HEADS = 4
SB_GROUP = SB_HEADS // SB_KV_HEADS
FFN_RES_SCALE = 0.5
NORM_EPS = 1e-6
LN_EPS = 1e-5
LOG2E = 1.4426950408889634
SOFTPLUS_CLAMP = 64.0
DEAD_RUN = -160.0
FINISHED_RUN = -1e30

VMEM_LIMIT_BYTES = 56 * 1024 * 1024

BF16_SUBLANE_TILE = 16
LANES = 128

TOKEN_TILE = 512
FFN_TILE = 1024
FFN_CHUNK = 256
ATT_BLOCK = 256
ATT_Q_SUB = 8

BF16 = jnp.bfloat16
F32 = jnp.float32


def _const_spec(shape):
    nd = len(shape)
    return pl.BlockSpec(shape, lambda *_: (0,) * nd, pipeline_mode=pl.Buffered(1))


def _rms_norm(x, g):
    return x * lax.rsqrt(jnp.mean(x * x, axis=-1, keepdims=True) + NORM_EPS) * g


def _gelu_tanh(x):
    c1 = math.sqrt(2.0 / math.pi)
    c2 = c1 * 0.044715
    return x * (0.5 + 0.5 * jnp.tanh(x * (c1 + c2 * (x * x))))


def _params(n_grid_dims, semantics="parallel"):
    return pltpu.CompilerParams(
        dimension_semantics=(semantics,) * n_grid_dims,
        vmem_limit_bytes=VMEM_LIMIT_BYTES)


class _Cast(NamedTuple):
    src: jax.Array
    lead: tuple
    rows: int
    cols: int


def _cast_io(casts, steps):
    in_specs, out_specs, out_shape, args = [], [], [], []
    for c in casts:
        n_slabs = max(n for n in range(1, steps + 1)
                      if c.rows % n == 0 and (c.rows // n) % BF16_SUBLANE_TILE == 0)
        slab = c.rows // n_slabs
        lead = c.lead
        in_specs.append(pl.BlockSpec(
            (None,) * len(lead) + (slab, c.cols),
            lambda i, lead=lead, last=n_slabs - 1: lead + (jnp.minimum(i, last), 0)))
        out_specs.append(pl.BlockSpec(
            (slab, c.cols), lambda i, last=n_slabs - 1: (jnp.minimum(i, last), 0)))
        out_shape.append(jax.ShapeDtypeStruct((c.rows, c.cols), BF16))
        args.append(c.src)
    return in_specs, out_specs, out_shape, args


def _do_casts(cast_in_refs, cast_out_refs):
    for src, dst in zip(cast_in_refs, cast_out_refs):
        dst[...] = src[...].astype(BF16)


class _FfnPlan(NamedTuple):
    attn_in: bool
    final_norm: bool
    proj_outs: int
    proj_scale: float
    n_casts: int


def _ffn_kernel(*refs, plan):
    it = iter(refs)
    x_ref = next(it)
    if plan.attn_in:
        o_ref, wo_ref = next(it), next(it)
    g_ref, win_ref, wout_ref = next(it), next(it), next(it)
    if plan.final_norm:
        fg_ref = next(it)
    if plan.proj_outs:
        pg_ref, pw_ref = next(it), next(it)
    cast_in_refs = [next(it) for _ in range(plan.n_casts)]
    y_ref = next(it)
    proj_refs = [next(it) for _ in range(plan.proj_outs)]
    _do_casts(cast_in_refs, list(it))

    x = x_ref[...]
    if plan.attn_in:
        x = x + jnp.dot(o_ref[...], wo_ref[...], preferred_element_type=F32)
    h = _rms_norm(x, g_ref[...]).astype(BF16)
    acc = jnp.zeros(x.shape, F32)
    for c in range(D_FF // FFN_CHUNK):
        lo = c * FFN_CHUNK
        gate = jnp.dot(h, win_ref[:, lo:lo + FFN_CHUNK], preferred_element_type=F32)
        up = jnp.dot(h, win_ref[:, D_FF + lo:D_FF + lo + FFN_CHUNK], preferred_element_type=F32)
        act = (gate * jax.nn.sigmoid(gate) * up).astype(BF16)
        acc = acc + jnp.dot(act, wout_ref[lo:lo + FFN_CHUNK, :], preferred_element_type=F32)
    y = x + FFN_RES_SCALE * acc
    if plan.proj_outs:
        p = jnp.dot(_rms_norm(y, pg_ref[...]).astype(BF16), pw_ref[...],
                    preferred_element_type=F32)
        if plan.proj_scale != 1.0:
            p = p * plan.proj_scale
        p = p.astype(BF16)
        heads = proj_refs[0].shape[1]
        for oi, p_ref in enumerate(proj_refs):
            for hd in range(heads):
                c0 = (oi * heads + hd) * SB_HEAD_DIM
                p_ref[0, hd] = p[:, c0:c0 + SB_HEAD_DIM]
    if plan.final_norm:
        y = _rms_norm(y, fg_ref[...])
    y_ref[...] = y


def _ffn(x2, seq_len, gain, w_in, w_out, name, *, attn=None, final_g=None, proj=None, casts=()):
    T = x2.shape[0]
    plan = _FfnPlan(attn is not None, final_g is not None,
                    0 if proj is None else proj[2], 1.0 if proj is None else proj[3], len(casts))
    tm = TOKEN_TILE if plan.proj_outs > 1 else FFN_TILE
    row = pl.BlockSpec((tm, D_MODEL), lambda i: (i, 0))

    in_specs, args = [row], [x2]
    if plan.attn_in:
        o2, w_o = attn
        in_specs += [row, _const_spec(w_o.shape)]
        args += [o2, w_o]
    in_specs += [_const_spec((1, D_MODEL)), _const_spec(w_in.shape), _const_spec(w_out.shape)]
    args += [gain.reshape(1, D_MODEL), w_in, w_out]
    if plan.final_norm:
        in_specs.append(_const_spec((1, D_MODEL)))
        args.append(final_g.reshape(1, D_MODEL))
    out_specs, out_shape = [row], [jax.ShapeDtypeStruct((T, D_MODEL), F32)]
    if plan.proj_outs:
        p_g, p_w = proj[0], proj[1]
        in_specs += [_const_spec((1, D_MODEL)), _const_spec(p_w.shape)]
        args += [p_g.reshape(1, D_MODEL), p_w]
        heads = p_w.shape[1] // (plan.proj_outs * SB_HEAD_DIM)
        tiles_per_seq = seq_len // tm
        out_specs += [pl.BlockSpec((1, heads, tm, SB_HEAD_DIM),
                                   lambda i: (i // tiles_per_seq, 0, i % tiles_per_seq, 0))
                      ] * plan.proj_outs
        out_shape += [jax.ShapeDtypeStruct((T // seq_len, heads, seq_len, SB_HEAD_DIM), BF16)
                      ] * plan.proj_outs
    c_in, c_out, c_shape, c_args = _cast_io(casts, T // tm)
    return pl.pallas_call(
        functools.partial(_ffn_kernel, plan=plan),
        grid=(T // tm,),
        in_specs=in_specs + c_in,
        out_specs=out_specs + c_out,
        out_shape=out_shape + c_shape,
        compiler_params=_params(1, "arbitrary"),
        name=name,
    )(*args, *c_args)


def _sgu_kernel(x_ref, g_ref, win_ref, lng_ref, lnb_ref, ws_ref, bs_ref, wout_ref, *rest):
    n_casts = (len(rest) - 1) // 2
    o_ref = rest[n_casts]
    _do_casts(rest[:n_casts], rest[n_casts + 1:])

    x = x_ref[...]
    tm = x.shape[0]
    h = _rms_norm(x, g_ref[...]).astype(BF16)
    def in_proj(c0):
        return jnp.dot(h, win_ref[:, c0:c0 + SGU_GROUP_DIM], preferred_element_type=F32)

    v = jnp.concatenate([_gelu_tanh(in_proj(D_A + gi * SGU_GROUP_DIM))
                         for gi in range(SGU_GROUPS)], axis=1)
    u_raw = [in_proj(gi * SGU_GROUP_DIM) for gi in range(SGU_GROUPS)]
    mu = jnp.mean(v, axis=-1, keepdims=True)
    vc = v - mu
    var = jnp.mean(vc * vc, axis=-1, keepdims=True)
    vn = (vc * lax.rsqrt(var + LN_EPS) * lng_ref[...] + lnb_ref[...]).astype(BF16)

    row = lax.broadcasted_iota(jnp.int32, (CHUNK, CHUNK), 0)
    col = lax.broadcasted_iota(jnp.int32, (CHUNK, CHUNK), 1)
    causal = row >= col
    bs = bs_ref[...]

    def spatial(gi):
        ws = jnp.where(causal, ws_ref[gi], 0.0).astype(BF16)
        bias = bs[:, gi:gi + 1]
        c0 = gi * SGU_GROUP_DIM
        return jnp.concatenate(
            [jnp.dot(ws, vn[n * CHUNK:(n + 1) * CHUNK, c0:c0 + SGU_GROUP_DIM],
                     preferred_element_type=F32) + bias for n in range(tm // CHUNK)], axis=0)

    acc = jnp.zeros(x.shape, F32)
    vs_next = spatial(0)
    for gi in range(SGU_GROUPS):
        vs = vs_next
        if gi + 1 < SGU_GROUPS:
            vs_next = spatial(gi + 1)
        gated = (_gelu_tanh(u_raw[gi]) * vs).astype(BF16)
        acc = acc + jnp.dot(gated, wout_ref[gi * SGU_GROUP_DIM:(gi + 1) * SGU_GROUP_DIM, :],
                            preferred_element_type=F32)
    o_ref[...] = x + acc


def _sgu(x2, g, w_in, ln_g, ln_b, w_s, b_s, w_out, casts=()):
    T = x2.shape[0]
    tm = TOKEN_TILE
    row = pl.BlockSpec((tm, D_MODEL), lambda i: (i, 0))
    c_in, c_out, c_shape, c_args = _cast_io(casts, T // tm)
    return pl.pallas_call(
        _sgu_kernel,
        grid=(T // tm,),
        in_specs=[row, _const_spec((1, D_MODEL)), _const_spec(w_in.shape),
                  _const_spec((1, D_A)), _const_spec((1, D_A)),
                  _const_spec(w_s.shape), _const_spec((CHUNK, SGU_GROUPS)),
                  _const_spec(w_out.shape)] + c_in,
        out_specs=[row] + c_out,
        out_shape=[jax.ShapeDtypeStruct((T, D_MODEL), F32)] + c_shape,
        compiler_params=_params(1, "arbitrary"),
        name="sgu",
    )(x2, g.reshape(1, D_MODEL), w_in, ln_g.reshape(1, D_A), ln_b.reshape(1, D_A),
      w_s, b_s.T, w_out, *c_args)


def _attn_kernel(q_ref, k_ref, v_ref, o_ref, acc_ref, run_ref):
    qt = pl.program_id(2)
    blk = ATT_BLOCK
    nsub = ATT_Q_SUB
    row = lax.broadcasted_iota(jnp.int32, (blk, blk), 0)
    col = lax.broadcasted_iota(jnp.int32, (blk, blk), 1)
    suffix = jnp.where(row > col, -1.0, 0.0).astype(BF16)
    strictly_earlier = col < row

    def load_kv(j):
        rows = pl.ds(pl.multiple_of(jnp.maximum(j, 0) * blk, blk), blk)
        return k_ref[0, 0, rows, :], v_ref[0, 0, rows, :]

    def finished_after(j):
        return jnp.where(j >= 1, 0.0, FINISHED_RUN).astype(F32)

    def scores(unit):
        sub, hd, (kb, _), diag, retire = unit
        idx = sub * SB_GROUP + hd
        z = lax.dot_general(q_ref[0, hd, sub * blk:(sub + 1) * blk, :], kb,
                            (((1,), (1,)), ((), ())), preferred_element_type=F32)
        sp = jnp.maximum(z, jnp.log2(1.0 + jnp.exp2(jnp.minimum(z, SOFTPLUS_CLAMP))))
        if diag:
            sp = jnp.where(strictly_earlier, sp, 0.0)
        run = 0.0 if diag else run_ref[idx]
        new_run = run - jnp.broadcast_to(jnp.sum(sp, axis=-1, keepdims=True), (blk, LANES))
        if retire is not None:
            new_run = new_run + retire
        run_ref[idx] = new_run
        w = z - sp
        if not diag:
            w = w + jnp.concatenate([run, run], axis=1)
        return sp.astype(BF16), w, new_run

    def weights(unit, st):
        sp, w = st
        a = jnp.exp2(w + jnp.dot(sp, suffix, preferred_element_type=F32))
        if unit[3]:
            a = jnp.where(strictly_earlier, a, 0.0)
        return a.astype(BF16)

    def accumulate(unit, a):
        sub, hd, (_, vb), diag, _ = unit
        idx = sub * SB_GROUP + hd
        pv = jnp.dot(a, vb, preferred_element_type=F32)
        if diag:
            acc_ref[idx] = pv
        else:
            acc_ref[idx] += pv

    def run_units(units, tracked):
        n = len(units)
        st, a = {}, {}
        max_run = None
        for step in range(n + 2):
            if step < n:
                sp, w, new_run = scores(units[step])
                st[step] = (sp, w)
                if step in tracked:
                    max_run = new_run if max_run is None else jnp.maximum(max_run, new_run)
            if 0 <= step - 1 < n:
                a[step - 1] = weights(units[step - 1], st.pop(step - 1))
            if step - 2 >= 0:
                accumulate(units[step - 2], a.pop(step - 2))
        return max_run

    def alive(max_run):
        return (jnp.max(max_run) > DEAD_RUN).astype(jnp.int32)

    first = qt * nsub
    units, tracked = [], set()
    for off in range(nsub - 1, -2, -1):
        j = first + off
        kv = load_kv(j)
        retire = finished_after(j) if off <= 0 else None
        for sub in (off, off + 1):
            if 0 <= sub < nsub:
                for hd in range(SB_GROUP):
                    if sub == off + 1:
                        tracked.add(len(units))
                    units.append((sub, hd, kv, sub == off, retire))
    head_max = run_units(units, tracked)

    n_rounds = first + nsub - 2

    def more_rounds(carry):
        i, live = carry
        return jnp.logical_and(i < n_rounds, live > 0)

    def one_round(carry):
        i, _ = carry
        units = []
        for sub in range(nsub):
            j = first + sub - 2 - i
            kv = load_kv(j)
            retire = finished_after(j)
            units += [(sub, hd, kv, False, retire) for hd in range(SB_GROUP)]
        return i + 1, alive(run_units(units, set(range(len(units)))))

    lax.while_loop(more_rounds, one_round, (jnp.int32(0), alive(head_max)))

    for sub in range(nsub):
        o_ref[0, sub * blk:(sub + 1) * blk, :] = jnp.concatenate(
            [acc_ref[sub * SB_GROUP + hd] for hd in range(SB_GROUP)], axis=1).astype(o_ref.dtype)


def _attention(q, k, v):
    B, _, S, _ = q.shape
    tq = ATT_BLOCK * ATT_Q_SUB
    kv_spec = pl.BlockSpec((1, 1, S, SB_HEAD_DIM), lambda b, kh, i: (b, kh, 0, 0))
    return pl.pallas_call(
        _attn_kernel,
        grid=(B, SB_KV_HEADS, S // tq),
        in_specs=[pl.BlockSpec((1, SB_GROUP, tq, SB_HEAD_DIM), lambda b, kh, i: (b, kh, i, 0)),
                  kv_spec, kv_spec],
        out_specs=pl.BlockSpec((1, tq, SB_GROUP * SB_HEAD_DIM), lambda b, kh, i: (b, i, kh)),
        out_shape=jax.ShapeDtypeStruct((B, S, SB_HEADS * SB_HEAD_DIM), BF16),
        scratch_shapes=[pltpu.VMEM((ATT_Q_SUB * SB_GROUP, ATT_BLOCK, SB_HEAD_DIM), F32),
                        pltpu.VMEM((ATT_Q_SUB * SB_GROUP, ATT_BLOCK, LANES), F32)],
        compiler_params=_params(3),
        name="sb_attention",
    )(q, k, v)


def kernel(x, ffn_norm, ffn_w_in, ffn_w_out, mix_norm, a_w_in, a_ln_g, a_ln_b, a_w_s, a_b_s,
           a_w_out, kv_norm, w_kv, b_w_q, b_w_o, final_norm):
    B, S, D = x.shape
    depth = ffn_norm.shape[0]
    n_a = a_w_in.shape[0]
    assert 1 <= n_a < depth, "shared K/V are produced by the last mixer-A layer"
    q_scale = LOG2E / math.sqrt(SB_HEAD_DIM)

    def ffn_casts(layer, half):
        return [_Cast(ffn_w_in, (layer, half), D, 2 * D_FF),
                _Cast(ffn_w_out, (layer, half), D_FF, D)]

    def plain(w):
        return _Cast(w, (), w.shape[0], w.shape[1])

    w_in, w_out = ffn_w_in[0, 0].astype(BF16), ffn_w_out[0, 0].astype(BF16)
    x2 = x.reshape(B * S, D)
    k = v = None
    for i in range(depth):
        if i < n_a:
            x2, a_in, a_out = _ffn(x2, S, ffn_norm[i, 0], w_in, w_out, "ffn_l%d_h0" % i,
                                   casts=[plain(a_w_in[i]), plain(a_w_out[i])])
            outs = _sgu(x2, mix_norm[i], a_in, a_ln_g[i], a_ln_b[i], a_w_s[i], a_b_s[i], a_out,
                        casts=ffn_casts(i, 1) + ([plain(w_kv)] if i == n_a - 1 else []))
            x2, w_in, w_out = outs[:3]
            if i == n_a - 1:
                x2, k, v, w_in, w_out, w_q = _ffn(
                    x2, S, ffn_norm[i, 1], w_in, w_out, "ffn_l%d_h1" % i,
                    proj=(kv_norm, outs[3], 2, 1.0),
                    casts=ffn_casts(i + 1, 0) + [plain(b_w_q[0])])
            else:
                x2, w_in, w_out = _ffn(x2, S, ffn_norm[i, 1], w_in, w_out, "ffn_l%d_h1" % i,
                                       casts=ffn_casts(i + 1, 0))
        else:
            j = i - n_a
            x2, q, w_in, w_out, w_o = _ffn(
                x2, S, ffn_norm[i, 0], w_in, w_out, "ffn_l%d_h0" % i,
                proj=(mix_norm[i], w_q, 1, q_scale),
                casts=ffn_casts(i, 1) + [plain(b_w_o[j])])
            o = _attention(q, k, v)
            last = i == depth - 1
            casts = [] if last else ffn_casts(i + 1, 0) + [plain(b_w_q[j + 1])]
            outs = _ffn(x2, S, ffn_norm[i, 1], w_in, w_out, "ffn_l%d_h1" % i,
                        attn=(o.reshape(B * S, D), w_o),
                        final_g=final_norm if last else None, casts=casts)
            x2 = outs[0]
            if not last:
                w_in, w_out, w_q = outs[1:4]
    return x2.reshape(B, S, D)
```

```python
import functools
import math
from typing import NamedTuple

import jax
import jax.numpy as jnp
from jax import lax
from jax.experimental import pallas as pl
from jax.experimental.pallas import tpu as pltpu

D_MODEL = 1024
D_FF = 2816
D_A = 2 * D_MODEL
CHUNK = 128
SGU_GROUPS = 8
SGU_GROUP_DIM = D_A // SGU_GROUPS
SB_HEADS = 16
SB_HEAD_DIM = 64
SB_KV_HEADS = 4
SB_GROUP = SB_HEADS // SB_KV_HEADS
FFN_RES_SCALE = 0.5
NORM_EPS = 1e-6
LN_EPS = 1e-5
LOG2E = 1.4426950408889634
SOFTPLUS_CLAMP = 64.0
DEAD_RUN = float("-inf")
FINISHED_RUN = -1e30

VMEM_LIMIT_BYTES = 56 * 1024 * 1024

BF16_SUBLANE_TILE = 16
LANES = 128

TOKEN_TILE = 512
FFN_TILE = 1024
FFN_CHUNK = 256
ATT_BLOCK = 256
ATT_Q_SUB = 8

BF16 = jnp.bfloat16
F32 = jnp.float32


def _const_spec(shape):
    nd = len(shape)
    return pl.BlockSpec(shape, lambda *_: (0,) * nd, pipeline_mode=pl.Buffered(1))


def _rms_norm(x, g):
    return x * lax.rsqrt(jnp.mean(x * x, axis=-1, keepdims=True) + NORM_EPS) * g


def _gelu_tanh(x):
    c1 = math.sqrt(2.0 / math.pi)
    c2 = c1 * 0.044715
    return x * (0.5 + 0.5 * jnp.tanh(x * (c1 + c2 * (x * x))))


def _params(n_grid_dims, semantics="parallel"):
    return pltpu.CompilerParams(
        dimension_semantics=(semantics,) * n_grid_dims,
        vmem_limit_bytes=VMEM_LIMIT_BYTES)


class _Cast(NamedTuple):
    src: jax.Array
    lead: tuple
    rows: int
    cols: int


def _cast_io(casts, steps):
    in_specs, out_specs, out_shape, args = [], [], [], []
    for c in casts:
        n_slabs = max(n for n in range(1, steps + 1)
                      if c.rows % n == 0 and (c.rows // n) % BF16_SUBLANE_TILE == 0)
        slab = c.rows // n_slabs
        lead = c.lead
        in_specs.append(pl.BlockSpec(
            (None,) * len(lead) + (slab, c.cols),
            lambda i, lead=lead, last=n_slabs - 1: lead + (jnp.minimum(i, last), 0)))
        out_specs.append(pl.BlockSpec(
            (slab, c.cols), lambda i, last=n_slabs - 1: (jnp.minimum(i, last), 0)))
        out_shape.append(jax.ShapeDtypeStruct((c.rows, c.cols), BF16))
        args.append(c.src)
    return in_specs, out_specs, out_shape, args


def _do_casts(cast_in_refs, cast_out_refs):
    for src, dst in zip(cast_in_refs, cast_out_refs):
        dst[...] = src[...].astype(BF16)


class _FfnPlan(NamedTuple):
    attn_in: bool
    final_norm: bool
    proj_outs: int
    proj_scale: float
    n_casts: int


def _ffn_kernel(*refs, plan):
    it = iter(refs)
    x_ref = next(it)
    if plan.attn_in:
        o_ref, wo_ref = next(it), next(it)
    g_ref, win_ref, wout_ref = next(it), next(it), next(it)
    if plan.final_norm:
        fg_ref = next(it)
    if plan.proj_outs:
        pg_ref, pw_ref = next(it), next(it)
    cast_in_refs = [next(it) for _ in range(plan.n_casts)]
    y_ref = next(it)
    proj_refs = [next(it) for _ in range(plan.proj_outs)]
    _do_casts(cast_in_refs, list(it))

    x = x_ref[...]
    if plan.attn_in:
        x = x + jnp.dot(o_ref[...], wo_ref[...], preferred_element_type=F32)
    h = _rms_norm(x, g_ref[...]).astype(BF16)
    acc = jnp.zeros(x.shape, F32)
    for c in range(D_FF // FFN_CHUNK):
        lo = c * FFN_CHUNK
        gate = jnp.dot(h, win_ref[:, lo:lo + FFN_CHUNK], preferred_element_type=F32)
        up = jnp.dot(h, win_ref[:, D_FF + lo:D_FF + lo + FFN_CHUNK], preferred_element_type=F32)
        act = (gate * jax.nn.sigmoid(gate) * up).astype(BF16)
        acc = acc + jnp.dot(act, wout_ref[lo:lo + FFN_CHUNK, :], preferred_element_type=F32)
    y = x + FFN_RES_SCALE * acc
    if plan.proj_outs:
        p = jnp.dot(_rms_norm(y, pg_ref[...]).astype(BF16), pw_ref[...],
                    preferred_element_type=F32)
        if plan.proj_scale != 1.0:
            p = p * plan.proj_scale
        p = p.astype(BF16)
        heads = proj_refs[0].shape[1]
        for oi, p_ref in enumerate(proj_refs):
            for hd in range(heads):
                c0 = (oi * heads + hd) * SB_HEAD_DIM
                p_ref[0, hd] = p[:, c0:c0 + SB_HEAD_DIM]
    if plan.final_norm:
        y = _rms_norm(y, fg_ref[...])
    y_ref[...] = y


def _ffn(x2, seq_len, gain, w_in, w_out, name, *, attn=None, final_g=None, proj=None, casts=()):
    T = x2.shape[0]
    plan = _FfnPlan(attn is not None, final_g is not None,
                    0 if proj is None else proj[2], 1.0 if proj is None else proj[3], len(casts))
    tm = TOKEN_TILE if plan.proj_outs > 1 else FFN_TILE
    row = pl.BlockSpec((tm, D_MODEL), lambda i: (i, 0))

    in_specs, args = [row], [x2]
    if plan.attn_in:
        o2, w_o = attn
        in_specs += [row, _const_spec(w_o.shape)]
        args += [o2, w_o]
    in_specs += [_const_spec((1, D_MODEL)), _const_spec(w_in.shape), _const_spec(w_out.shape)]
    args += [gain.reshape(1, D_MODEL), w_in, w_out]
    if plan.final_norm:
        in_specs.append(_const_spec((1, D_MODEL)))
        args.append(final_g.reshape(1, D_MODEL))
    out_specs, out_shape = [row], [jax.ShapeDtypeStruct((T, D_MODEL), F32)]
    if plan.proj_outs:
        p_g, p_w = proj[0], proj[1]
        in_specs += [_const_spec((1, D_MODEL)), _const_spec(p_w.shape)]
        args += [p_g.reshape(1, D_MODEL), p_w]
        heads = p_w.shape[1] // (plan.proj_outs * SB_HEAD_DIM)
        tiles_per_seq = seq_len // tm
        out_specs += [pl.BlockSpec((1, heads, tm, SB_HEAD_DIM),
                                   lambda i: (i // tiles_per_seq, 0, i % tiles_per_seq, 0))
                      ] * plan.proj_outs
        out_shape += [jax.ShapeDtypeStruct((T // seq_len, heads, seq_len, SB_HEAD_DIM), BF16)
                      ] * plan.proj_outs
    c_in, c_out, c_shape, c_args = _cast_io(casts, T // tm)
    return pl.pallas_call(
        functools.partial(_ffn_kernel, plan=plan),
        grid=(T // tm,),
        in_specs=in_specs + c_in,
        out_specs=out_specs + c_out,
        out_shape=out_shape + c_shape,
        compiler_params=_params(1, "arbitrary"),
        name=name,
    )(*args, *c_args)


def _sgu_kernel(x_ref, g_ref, win_ref, lng_ref, lnb_ref, ws_ref, bs_ref, wout_ref, *rest):
    n_casts = (len(rest) - 1) // 2
    o_ref = rest[n_casts]
    _do_casts(rest[:n_casts], rest[n_casts + 1:])

    x = x_ref[...]
    tm = x.shape[0]
    h = _rms_norm(x, g_ref[...]).astype(BF16)
    def in_proj(c0):
        return jnp.dot(h, win_ref[:, c0:c0 + SGU_GROUP_DIM], preferred_element_type=F32)

    v = jnp.concatenate([_gelu_tanh(in_proj(D_A + gi * SGU_GROUP_DIM))
                         for gi in range(SGU_GROUPS)], axis=1)
    u_raw = [in_proj(gi * SGU_GROUP_DIM) for gi in range(SGU_GROUPS)]
    mu = jnp.mean(v, axis=-1, keepdims=True)
    vc = v - mu
    var = jnp.mean(vc * vc, axis=-1, keepdims=True)
    vn = (vc * lax.rsqrt(var + LN_EPS) * lng_ref[...] + lnb_ref[...]).astype(BF16)

    row = lax.broadcasted_iota(jnp.int32, (CHUNK, CHUNK), 0)
    col = lax.broadcasted_iota(jnp.int32, (CHUNK, CHUNK), 1)
    causal = row >= col
    bs = bs_ref[...]

    def spatial(gi):
        ws = jnp.where(causal, ws_ref[gi], 0.0).astype(BF16)
        bias = bs[:, gi:gi + 1]
        c0 = gi * SGU_GROUP_DIM
        return jnp.concatenate(
            [jnp.dot(ws, vn[n * CHUNK:(n + 1) * CHUNK, c0:c0 + SGU_GROUP_DIM],
                     preferred_element_type=F32) + bias for n in range(tm // CHUNK)], axis=0)

    acc = jnp.zeros(x.shape, F32)
    vs_next = spatial(0)
    for gi in range(SGU_GROUPS):
        vs = vs_next
        if gi + 1 < SGU_GROUPS:
            vs_next = spatial(gi + 1)
        gated = (_gelu_tanh(u_raw[gi]) * vs).astype(BF16)
        acc = acc + jnp.dot(gated, wout_ref[gi * SGU_GROUP_DIM:(gi + 1) * SGU_GROUP_DIM, :],
                            preferred_element_type=F32)
    o_ref[...] = x + acc


def _sgu(x2, g, w_in, ln_g, ln_b, w_s, b_s, w_out, casts=()):
    T = x2.shape[0]
    tm = TOKEN_TILE
    row = pl.BlockSpec((tm, D_MODEL), lambda i: (i, 0))
    c_in, c_out, c_shape, c_args = _cast_io(casts, T // tm)
    return pl.pallas_call(
        _sgu_kernel,
        grid=(T // tm,),
        in_specs=[row, _const_spec((1, D_MODEL)), _const_spec(w_in.shape),
                  _const_spec((1, D_A)), _const_spec((1, D_A)),
                  _const_spec(w_s.shape), _const_spec((CHUNK, SGU_GROUPS)),
                  _const_spec(w_out.shape)] + c_in,
        out_specs=[row] + c_out,
        out_shape=[jax.ShapeDtypeStruct((T, D_MODEL), F32)] + c_shape,
        compiler_params=_params(1, "arbitrary"),
        name="sgu",
    )(x2, g.reshape(1, D_MODEL), w_in, ln_g.reshape(1, D_A), ln_b.reshape(1, D_A),
      w_s, b_s.T, w_out, *c_args)


def _attn_kernel(q_ref, k_ref, v_ref, o_ref, acc_ref, run_ref):
    qt = pl.program_id(2)
    blk = ATT_BLOCK
    nsub = ATT_Q_SUB
    row = lax.broadcasted_iota(jnp.int32, (blk, blk), 0)
    col = lax.broadcasted_iota(jnp.int32, (blk, blk), 1)
    suffix = jnp.where(row > col, -1.0, 0.0).astype(BF16)
    strictly_earlier = col < row

    def load_kv(j):
        rows = pl.ds(pl.multiple_of(jnp.maximum(j, 0) * blk, blk), blk)
        return k_ref[0, 0, rows, :], v_ref[0, 0, rows, :]

    def finished_after(j):
        return jnp.where(j >= 1, 0.0, FINISHED_RUN).astype(F32)

    def scores(unit):
        sub, hd, (kb, _), diag, retire = unit
        idx = sub * SB_GROUP + hd
        z = lax.dot_general(q_ref[0, hd, sub * blk:(sub + 1) * blk, :], kb,
                            (((1,), (1,)), ((), ())), preferred_element_type=F32)
        sp = jnp.maximum(z, jnp.log2(1.0 + jnp.exp2(jnp.minimum(z, SOFTPLUS_CLAMP))))
        if diag:
            sp = jnp.where(strictly_earlier, sp, 0.0)
        run = 0.0 if diag else run_ref[idx]
        new_run = run - jnp.broadcast_to(jnp.sum(sp, axis=-1, keepdims=True), (blk, LANES))
        if retire is not None:
            new_run = new_run + retire
        run_ref[idx] = new_run
        w = z - sp
        if not diag:
            w = w + jnp.concatenate([run, run], axis=1)
        return sp.astype(BF16), w, new_run

    def weights(unit, st):
        sp, w = st
        a = jnp.exp2(w + jnp.dot(sp, suffix, preferred_element_type=F32))
        if unit[3]:
            a = jnp.where(strictly_earlier, a, 0.0)
        return a.astype(BF16)

    def accumulate(unit, a):
        sub, hd, (_, vb), diag, _ = unit
        idx = sub * SB_GROUP + hd
        pv = jnp.dot(a, vb, preferred_element_type=F32)
        if diag:
            acc_ref[idx] = pv
        else:
            acc_ref[idx] += pv

    def run_units(units, tracked):
        n = len(units)
        st, a = {}, {}
        max_run = None
        for step in range(n + 2):
            if step < n:
                sp, w, new_run = scores(units[step])
                st[step] = (sp, w)
                if step in tracked:
                    max_run = new_run if max_run is None else jnp.maximum(max_run, new_run)
            if 0 <= step - 1 < n:
                a[step - 1] = weights(units[step - 1], st.pop(step - 1))
            if step - 2 >= 0:
                accumulate(units[step - 2], a.pop(step - 2))
        return max_run

    def alive(max_run):
        return (jnp.max(max_run) > DEAD_RUN).astype(jnp.int32)

    first = qt * nsub
    units, tracked = [], set()
    for off in range(nsub - 1, -2, -1):
        j = first + off
        kv = load_kv(j)
        retire = finished_after(j) if off <= 0 else None
        for sub in (off, off + 1):
            if 0 <= sub < nsub:
                for hd in range(SB_GROUP):
                    if sub == off + 1:
                        tracked.add(len(units))
                    units.append((sub, hd, kv, sub == off, retire))
    head_max = run_units(units, tracked)

    n_rounds = first + nsub - 2

    def more_rounds(carry):
        i, live = carry
        return jnp.logical_and(i < n_rounds, live > 0)

    def one_round(carry):
        i, _ = carry
        units = []
        for sub in range(nsub):
            j = first + sub - 2 - i
            kv = load_kv(j)
            retire = finished_after(j)
            units += [(sub, hd, kv, False, retire) for hd in range(SB_GROUP)]
        return i + 1, alive(run_units(units, set(range(len(units)))))

    lax.while_loop(more_rounds, one_round, (jnp.int32(0), alive(head_max)))

    for sub in range(nsub):
        o_ref[0, sub * blk:(sub + 1) * blk, :] = jnp.concatenate(
            [acc_ref[sub * SB_GROUP + hd] for hd in range(SB_GROUP)], axis=1).astype(o_ref.dtype)


def _attention(q, k, v):
    B, _, S, _ = q.shape
    tq = ATT_BLOCK * ATT_Q_SUB
    kv_spec = pl.BlockSpec((1, 1, S, SB_HEAD_DIM), lambda b, kh, i: (b, kh, 0, 0))
    return pl.pallas_call(
        _attn_kernel,
        grid=(B, SB_KV_HEADS, S // tq),
        in_specs=[pl.BlockSpec((1, SB_GROUP, tq, SB_HEAD_DIM), lambda b, kh, i: (b, kh, i, 0)),
                  kv_spec, kv_spec],
        out_specs=pl.BlockSpec((1, tq, SB_GROUP * SB_HEAD_DIM), lambda b, kh, i: (b, i, kh)),
        out_shape=jax.ShapeDtypeStruct((B, S, SB_HEADS * SB_HEAD_DIM), BF16),
        scratch_shapes=[pltpu.VMEM((ATT_Q_SUB * SB_GROUP, ATT_BLOCK, SB_HEAD_DIM), F32),
                        pltpu.VMEM((ATT_Q_SUB * SB_GROUP, ATT_BLOCK, LANES), F32)],
        compiler_params=_params(3),
        name="sb_attention",
    )(q, k, v)


def kernel(x, ffn_norm, ffn_w_in, ffn_w_out, mix_norm, a_w_in, a_ln_g, a_ln_b, a_w_s, a_b_s,
           a_w_out, kv_norm, w_kv, b_w_q, b_w_o, final_norm):
    B, S, D = x.shape
    depth = ffn_norm.shape[0]
    n_a = a_w_in.shape[0]
    assert 1 <= n_a < depth, "shared K/V are produced by the last mixer-A layer"
    q_scale = LOG2E / math.sqrt(SB_HEAD_DIM)

    def ffn_casts(layer, half):
        return [_Cast(ffn_w_in, (layer, half), D, 2 * D_FF),
                _Cast(ffn_w_out, (layer, half), D_FF, D)]

    def plain(w):
        return _Cast(w, (), w.shape[0], w.shape[1])

    w_in, w_out = ffn_w_in[0, 0].astype(BF16), ffn_w_out[0, 0].astype(BF16)
    x2 = x.reshape(B * S, D)
    k = v = None
    for i in range(depth):
        if i < n_a:
            x2, a_in, a_out = _ffn(x2, S, ffn_norm[i, 0], w_in, w_out, "ffn_l%d_h0" % i,
                                   casts=[plain(a_w_in[i]), plain(a_w_out[i])])
            outs = _sgu(x2, mix_norm[i], a_in, a_ln_g[i], a_ln_b[i], a_w_s[i], a_b_s[i], a_out,
                        casts=ffn_casts(i, 1) + ([plain(w_kv)] if i == n_a - 1 else []))
            x2, w_in, w_out = outs[:3]
            if i == n_a - 1:
                x2, k, v, w_in, w_out, w_q = _ffn(
                    x2, S, ffn_norm[i, 1], w_in, w_out, "ffn_l%d_h1" % i,
                    proj=(kv_norm, outs[3], 2, 1.0),
                    casts=ffn_casts(i + 1, 0) + [plain(b_w_q[0])])
            else:
                x2, w_in, w_out = _ffn(x2, S, ffn_norm[i, 1], w_in, w_out, "ffn_l%d_h1" % i,
                                       casts=ffn_casts(i + 1, 0))
        else:
            j = i - n_a
            x2, q, w_in, w_out, w_o = _ffn(
                x2, S, ffn_norm[i, 0], w_in, w_out, "ffn_l%d_h0" % i,
                proj=(mix_norm[i], w_q, 1, q_scale),
                casts=ffn_casts(i, 1) + [plain(b_w_o[j])])
            o = _attention(q, k, v)
            last = i == depth - 1
            casts = [] if last else ffn_casts(i + 1, 0) + [plain(b_w_q[j + 1])]
            outs = _ffn(x2, S, ffn_norm[i, 1], w_in, w_out, "ffn_l%d_h1" % i,
                        attn=(o.reshape(B * S, D), w_o),
                        final_g=final_norm if last else None, casts=casts)
            x2 = outs[0]
            if not last:
                w_in, w_out, w_q = outs[1:4]
    return x2.reshape(B, S, D)
```

```python
import functools
import math
from typing import NamedTuple

import jax
import jax.numpy as jnp
from jax import lax
from jax.experimental import pallas as pl
from jax.experimental.pallas import tpu as pltpu

D_MODEL = 1024
D_FF = 2816
D_A = 2 * D_MODEL
CHUNK = 128
SGU_GROUPS = 8
SGU_GROUP_DIM = D_A // SGU_GROUPS
SB_HEADS = 16
SB_HEAD_DIM = 64
SB_KV_HEADS = 4
SB_GROUP = SB_HEADS // SB_KV_HEADS
FFN_RES_SCALE = 0.5
NORM_EPS = 1e-6
LN_EPS = 1e-5
LOG2E = 1.4426950408889634
SOFTPLUS_CLAMP = 64.0
DEAD_RUN = -160.0
FINISHED_RUN = -1e30

VMEM_LIMIT_BYTES = 56 * 1024 * 1024

BF16_SUBLANE_TILE = 16
LANES = 128

TOKEN_TILE = 512
FFN_TILE = 1024
FFN_CHUNK = 256
ATT_BLOCK = 256
ATT_Q_SUB = 8
ATT_HEADS_PER_UNIT = 2

BF16 = jnp.bfloat16
F32 = jnp.float32


def _const_spec(shape):
    nd = len(shape)
    return pl.BlockSpec(shape, lambda *_: (0,) * nd, pipeline_mode=pl.Buffered(1))


def _rms_norm(x, g):
    return x * lax.rsqrt(jnp.mean(x * x, axis=-1, keepdims=True) + NORM_EPS) * g


def _gelu_tanh(x):
    c1 = math.sqrt(2.0 / math.pi)
    c2 = c1 * 0.044715
    return x * (0.5 + 0.5 * jnp.tanh(x * (c1 + c2 * (x * x))))


def _params(n_grid_dims, semantics="parallel"):
    return pltpu.CompilerParams(
        dimension_semantics=(semantics,) * n_grid_dims,
        vmem_limit_bytes=VMEM_LIMIT_BYTES)


class _Cast(NamedTuple):
    src: jax.Array
    lead: tuple
    rows: int
    cols: int


def _cast_io(casts, steps):
    in_specs, out_specs, out_shape, args = [], [], [], []
    for c in casts:
        n_slabs = max(n for n in range(1, steps + 1)
                      if c.rows % n == 0 and (c.rows // n) % BF16_SUBLANE_TILE == 0)
        slab = c.rows // n_slabs
        lead = c.lead
        in_specs.append(pl.BlockSpec(
            (None,) * len(lead) + (slab, c.cols),
            lambda i, lead=lead, last=n_slabs - 1: lead + (jnp.minimum(i, last), 0)))
        out_specs.append(pl.BlockSpec(
            (slab, c.cols), lambda i, last=n_slabs - 1: (jnp.minimum(i, last), 0)))
        out_shape.append(jax.ShapeDtypeStruct((c.rows, c.cols), BF16))
        args.append(c.src)
    return in_specs, out_specs, out_shape, args


def _do_casts(cast_in_refs, cast_out_refs):
    for src, dst in zip(cast_in_refs, cast_out_refs):
        dst[...] = src[...].astype(BF16)


class _FfnPlan(NamedTuple):
    attn_in: bool
    final_norm: bool
    proj_outs: int
    proj_scale: float
    n_casts: int


def _ffn_kernel(*refs, plan):
    it = iter(refs)
    x_ref = next(it)
    if plan.attn_in:
        o_ref, wo_ref = next(it), next(it)
    g_ref, win_ref, wout_ref = next(it), next(it), next(it)
    if plan.final_norm:
        fg_ref = next(it)
    if plan.proj_outs:
        pg_ref, pw_ref = next(it), next(it)
    cast_in_refs = [next(it) for _ in range(plan.n_casts)]
    y_ref = next(it)
    proj_refs = [next(it) for _ in range(plan.proj_outs)]
    _do_casts(cast_in_refs, list(it))

    x = x_ref[...]
    if plan.attn_in:
        x = x + jnp.dot(o_ref[...], wo_ref[...], preferred_element_type=F32)
    h = _rms_norm(x, g_ref[...]).astype(BF16)
    acc = jnp.zeros(x.shape, F32)
    for c in range(D_FF // FFN_CHUNK):
        lo = c * FFN_CHUNK
        gate = jnp.dot(h, win_ref[:, lo:lo + FFN_CHUNK], preferred_element_type=F32)
        up = jnp.dot(h, win_ref[:, D_FF + lo:D_FF + lo + FFN_CHUNK], preferred_element_type=F32)
        act = (gate * jax.nn.sigmoid(gate) * up).astype(BF16)
        acc = acc + jnp.dot(act, wout_ref[lo:lo + FFN_CHUNK, :], preferred_element_type=F32)
    y = x + FFN_RES_SCALE * acc
    if plan.proj_outs:
        p = jnp.dot(_rms_norm(y, pg_ref[...]).astype(BF16), pw_ref[...],
                    preferred_element_type=F32)
        if plan.proj_scale != 1.0:
            p = p * plan.proj_scale
        p = p.astype(BF16)
        heads = proj_refs[0].shape[1]
        for oi, p_ref in enumerate(proj_refs):
            for hd in range(heads):
                c0 = (oi * heads + hd) * SB_HEAD_DIM
                p_ref[0, hd] = p[:, c0:c0 + SB_HEAD_DIM]
    if plan.final_norm:
        y = _rms_norm(y, fg_ref[...])
    y_ref[...] = y


def _ffn(x2, seq_len, gain, w_in, w_out, name, *, attn=None, final_g=None, proj=None, casts=()):
    T = x2.shape[0]
    plan = _FfnPlan(attn is not None, final_g is not None,
                    0 if proj is None else proj[2], 1.0 if proj is None else proj[3], len(casts))
    tm = TOKEN_TILE if plan.proj_outs > 1 else FFN_TILE
    row = pl.BlockSpec((tm, D_MODEL), lambda i: (i, 0))

    in_specs, args = [row], [x2]
    if plan.attn_in:
        o2, w_o = attn
        in_specs += [row, _const_spec(w_o.shape)]
        args += [o2, w_o]
    in_specs += [_const_spec((1, D_MODEL)), _const_spec(w_in.shape), _const_spec(w_out.shape)]
    args += [gain.reshape(1, D_MODEL), w_in, w_out]
    if plan.final_norm:
        in_specs.append(_const_spec((1, D_MODEL)))
        args.append(final_g.reshape(1, D_MODEL))
    out_specs, out_shape = [row], [jax.ShapeDtypeStruct((T, D_MODEL), F32)]
    if plan.proj_outs:
        p_g, p_w = proj[0], proj[1]
        in_specs += [_const_spec((1, D_MODEL)), _const_spec(p_w.shape)]
        args += [p_g.reshape(1, D_MODEL), p_w]
        heads = p_w.shape[1] // (plan.proj_outs * SB_HEAD_DIM)
        tiles_per_seq = seq_len // tm
        out_specs += [pl.BlockSpec((1, heads, tm, SB_HEAD_DIM),
                                   lambda i: (i // tiles_per_seq, 0, i % tiles_per_seq, 0))
                      ] * plan.proj_outs
        out_shape += [jax.ShapeDtypeStruct((T // seq_len, heads, seq_len, SB_HEAD_DIM), BF16)
                      ] * plan.proj_outs
    c_in, c_out, c_shape, c_args = _cast_io(casts, T // tm)
    return pl.pallas_call(
        functools.partial(_ffn_kernel, plan=plan),
        grid=(T // tm,),
        in_specs=in_specs + c_in,
        out_specs=out_specs + c_out,
        out_shape=out_shape + c_shape,
        compiler_params=_params(1, "arbitrary"),
        name=name,
    )(*args, *c_args)


def _sgu_kernel(x_ref, g_ref, win_ref, lng_ref, lnb_ref, ws_ref, bs_ref, wout_ref, *rest):
    n_casts = (len(rest) - 1) // 2
    o_ref = rest[n_casts]
    _do_casts(rest[:n_casts], rest[n_casts + 1:])

    x = x_ref[...]
    tm = x.shape[0]
    h = _rms_norm(x, g_ref[...]).astype(BF16)
    def in_proj(c0):
        return jnp.dot(h, win_ref[:, c0:c0 + SGU_GROUP_DIM], preferred_element_type=F32)

    v = jnp.concatenate([_gelu_tanh(in_proj(D_A + gi * SGU_GROUP_DIM))
                         for gi in range(SGU_GROUPS)], axis=1)
    u_raw = [in_proj(gi * SGU_GROUP_DIM) for gi in range(SGU_GROUPS)]
    mu = jnp.mean(v, axis=-1, keepdims=True)
    vc = v - mu
    var = jnp.mean(vc * vc, axis=-1, keepdims=True)
    vn = (vc * lax.rsqrt(var + LN_EPS) * lng_ref[...] + lnb_ref[...]).astype(BF16)

    row = lax.broadcasted_iota(jnp.int32, (CHUNK, CHUNK), 0)
    col = lax.broadcasted_iota(jnp.int32, (CHUNK, CHUNK), 1)
    causal = row >= col
    bs = bs_ref[...]

    def spatial(gi):
        ws = jnp.where(causal, ws_ref[gi], 0.0).astype(BF16)
        bias = bs[:, gi:gi + 1]
        c0 = gi * SGU_GROUP_DIM
        return jnp.concatenate(
            [jnp.dot(ws, vn[n * CHUNK:(n + 1) * CHUNK, c0:c0 + SGU_GROUP_DIM],
                     preferred_element_type=F32) + bias for n in range(tm // CHUNK)], axis=0)

    acc = jnp.zeros(x.shape, F32)
    vs_next = spatial(0)
    for gi in range(SGU_GROUPS):
        vs = vs_next
        if gi + 1 < SGU_GROUPS:
            vs_next = spatial(gi + 1)
        gated = (_gelu_tanh(u_raw[gi]) * vs).astype(BF16)
        acc = acc + jnp.dot(gated, wout_ref[gi * SGU_GROUP_DIM:(gi + 1) * SGU_GROUP_DIM, :],
                            preferred_element_type=F32)
    o_ref[...] = x + acc


def _sgu(x2, g, w_in, ln_g, ln_b, w_s, b_s, w_out, casts=()):
    T = x2.shape[0]
    tm = TOKEN_TILE
    row = pl.BlockSpec((tm, D_MODEL), lambda i: (i, 0))
    c_in, c_out, c_shape, c_args = _cast_io(casts, T // tm)
    return pl.pallas_call(
        _sgu_kernel,
        grid=(T // tm,),
        in_specs=[row, _const_spec((1, D_MODEL)), _const_spec(w_in.shape),
                  _const_spec((1, D_A)), _const_spec((1, D_A)),
                  _const_spec(w_s.shape), _const_spec((CHUNK, SGU_GROUPS)),
                  _const_spec(w_out.shape)] + c_in,
        out_specs=[row] + c_out,
        out_shape=[jax.ShapeDtypeStruct((T, D_MODEL), F32)] + c_shape,
        compiler_params=_params(1, "arbitrary"),
        name="sgu",
    )(x2, g.reshape(1, D_MODEL), w_in, ln_g.reshape(1, D_A), ln_b.reshape(1, D_A),
      w_s, b_s.T, w_out, *c_args)


def _attn_kernel(q_ref, k_ref, v_ref, o_ref, acc_ref, run_ref):
    qt = pl.program_id(2)
    blk = ATT_BLOCK
    nsub = ATT_Q_SUB
    row = lax.broadcasted_iota(jnp.int32, (blk, blk), 0)
    col = lax.broadcasted_iota(jnp.int32, (blk, blk), 1)
    suffix = jnp.where(row > col, -1.0, 0.0).astype(BF16)
    strictly_earlier = col < row

    def load_kv(j):
        rows = pl.ds(pl.multiple_of(jnp.maximum(j, 0) * blk, blk), blk)
        return k_ref[0, 0, rows, :], v_ref[0, 0, rows, :]

    def finished_after(j):
        return jnp.where(j >= 1, 0.0, FINISHED_RUN).astype(F32)

    per_unit = ATT_HEADS_PER_UNIT
    stacked_causal = jnp.concatenate([strictly_earlier] * per_unit, axis=0)

    def scores(unit):
        sub, first_head, kv, diag, retire = unit
        heads = range(first_head, first_head + per_unit)
        kb = kv[0]
        q = jnp.concatenate([q_ref[0, hd, sub * blk:(sub + 1) * blk, :] for hd in heads], axis=0)
        z = lax.dot_general(q, kb, (((1,), (1,)), ((), ())), preferred_element_type=F32)
        sp = jnp.maximum(z, jnp.log2(1.0 + jnp.exp2(jnp.minimum(z, SOFTPLUS_CLAMP))))
        if diag:
            sp = jnp.where(stacked_causal, sp, 0.0)
        run = 0.0 if diag else jnp.concatenate(
            [run_ref[sub * SB_GROUP + hd] for hd in heads], axis=0)
        new_run = run - jnp.broadcast_to(jnp.sum(sp, axis=-1, keepdims=True),
                                         (per_unit * blk, LANES))
        if retire is not None:
            new_run = new_run + retire
        for n, hd in enumerate(heads):
            run_ref[sub * SB_GROUP + hd] = new_run[n * blk:(n + 1) * blk]
        w = z - sp
        if not diag:
            w = w + jnp.concatenate([run, run], axis=1)
        return sp.astype(BF16), w, new_run

    def weights(unit, st):
        sp, w = st
        a = jnp.exp2(w + jnp.dot(sp, suffix, preferred_element_type=F32))
        if unit[3]:
            a = jnp.where(stacked_causal, a, 0.0)
        return a.astype(BF16)

    def accumulate(unit, a):
        sub, first_head, kv, diag, _ = unit
        pv = jnp.dot(a, kv[1], preferred_element_type=F32)
        for n in range(per_unit):
            idx = sub * SB_GROUP + first_head + n
            if diag:
                acc_ref[idx] = pv[n * blk:(n + 1) * blk]
            else:
                acc_ref[idx] += pv[n * blk:(n + 1) * blk]

    def run_units(units, tracked):
        n = len(units)
        st, a = {}, {}
        max_run = None
        for step in range(n + 2):
            if step < n:
                sp, w, new_run = scores(units[step])
                st[step] = (sp, w)
                if step in tracked:
                    max_run = new_run if max_run is None else jnp.maximum(max_run, new_run)
            if 0 <= step - 1 < n:
                a[step - 1] = weights(units[step - 1], st.pop(step - 1))
            if step - 2 >= 0:
                accumulate(units[step - 2], a.pop(step - 2))
        return max_run

    def alive(max_run):
        return (jnp.max(max_run) > DEAD_RUN).astype(jnp.int32)

    first = qt * nsub
    units, tracked = [], set()
    for off in range(nsub - 1, -2, -1):
        j = first + off
        kv = load_kv(j)
        retire = finished_after(j) if off <= 0 else None
        for sub in (off, off + 1):
            if 0 <= sub < nsub:
                for hd in range(0, SB_GROUP, per_unit):
                    if sub == off + 1:
                        tracked.add(len(units))
                    units.append((sub, hd, kv, sub == off, retire))
    head_max = run_units(units, tracked)

    n_rounds = first + nsub - 2

    def more_rounds(carry):
        i, live = carry
        return jnp.logical_and(i < n_rounds, live > 0)

    def one_round(carry):
        i, _ = carry
        units = []
        for sub in range(nsub):
            j = first + sub - 2 - i
            kv = load_kv(j)
            retire = finished_after(j)
            units += [(sub, hd, kv, False, retire) for hd in range(0, SB_GROUP, per_unit)]
        return i + 1, alive(run_units(units, set(range(len(units)))))

    lax.while_loop(more_rounds, one_round, (jnp.int32(0), alive(head_max)))

    for sub in range(nsub):
        o_ref[0, sub * blk:(sub + 1) * blk, :] = jnp.concatenate(
            [acc_ref[sub * SB_GROUP + hd] for hd in range(SB_GROUP)], axis=1).astype(o_ref.dtype)


def _attention(q, k, v):
    B, _, S, _ = q.shape
    tq = ATT_BLOCK * ATT_Q_SUB
    kv_spec = pl.BlockSpec((1, 1, S, SB_HEAD_DIM), lambda b, kh, i: (b, kh, 0, 0))
    return pl.pallas_call(
        _attn_kernel,
        grid=(B, SB_KV_HEADS, S // tq),
        in_specs=[pl.BlockSpec((1, SB_GROUP, tq, SB_HEAD_DIM), lambda b, kh, i: (b, kh, i, 0)),
                  kv_spec, kv_spec],
        out_specs=pl.BlockSpec((1, tq, SB_GROUP * SB_HEAD_DIM), lambda b, kh, i: (b, i, kh)),
        out_shape=jax.ShapeDtypeStruct((B, S, SB_HEADS * SB_HEAD_DIM), BF16),
        scratch_shapes=[pltpu.VMEM((ATT_Q_SUB * SB_GROUP, ATT_BLOCK, SB_HEAD_DIM), F32),
                        pltpu.VMEM((ATT_Q_SUB * SB_GROUP, ATT_BLOCK, LANES), F32)],
        compiler_params=_params(3),
        name="sb_attention",
    )(q, k, v)


def kernel(x, ffn_norm, ffn_w_in, ffn_w_out, mix_norm, a_w_in, a_ln_g, a_ln_b, a_w_s, a_b_s,
           a_w_out, kv_norm, w_kv, b_w_q, b_w_o, final_norm):
    B, S, D = x.shape
    depth = ffn_norm.shape[0]
    n_a = a_w_in.shape[0]
    assert 1 <= n_a < depth, "shared K/V are produced by the last mixer-A layer"
    q_scale = LOG2E / math.sqrt(SB_HEAD_DIM)

    def ffn_casts(layer, half):
        return [_Cast(ffn_w_in, (layer, half), D, 2 * D_FF),
                _Cast(ffn_w_out, (layer, half), D_FF, D)]

    def plain(w):
        return _Cast(w, (), w.shape[0], w.shape[1])

    w_in, w_out = ffn_w_in[0, 0].astype(BF16), ffn_w_out[0, 0].astype(BF16)
    x2 = x.reshape(B * S, D)
    k = v = None
    for i in range(depth):
        if i < n_a:
            x2, a_in, a_out = _ffn(x2, S, ffn_norm[i, 0], w_in, w_out, "ffn_l%d_h0" % i,
                                   casts=[plain(a_w_in[i]), plain(a_w_out[i])])
            outs = _sgu(x2, mix_norm[i], a_in, a_ln_g[i], a_ln_b[i], a_w_s[i], a_b_s[i], a_out,
                        casts=ffn_casts(i, 1) + ([plain(w_kv)] if i == n_a - 1 else []))
            x2, w_in, w_out = outs[:3]
            if i == n_a - 1:
                x2, k, v, w_in, w_out, w_q = _ffn(
                    x2, S, ffn_norm[i, 1], w_in, w_out, "ffn_l%d_h1" % i,
                    proj=(kv_norm, outs[3], 2, 1.0),
                    casts=ffn_casts(i + 1, 0) + [plain(b_w_q[0])])
            else:
                x2, w_in, w_out = _ffn(x2, S, ffn_norm[i, 1], w_in, w_out, "ffn_l%d_h1" % i,
                                       casts=ffn_casts(i + 1, 0))
        else:
            j = i - n_a
            x2, q, w_in, w_out, w_o = _ffn(
                x2, S, ffn_norm[i, 0], w_in, w_out, "ffn_l%d_h0" % i,
                proj=(mix_norm[i], w_q, 1, q_scale),
                casts=ffn_casts(i, 1) + [plain(b_w_o[j])])
            o = _attention(q, k, v)
            last = i == depth - 1
            casts = [] if last else ffn_casts(i + 1, 0) + [plain(b_w_q[j + 1])]
            outs = _ffn(x2, S, ffn_norm[i, 1], w_in, w_out, "ffn_l%d_h1" % i,
                        attn=(o.reshape(B * S, D), w_o),
                        final_g=final_norm if last else None, casts=casts)
            x2 = outs[0]
            if not last:
                w_in, w_out, w_q = outs[1:4]
    return x2.reshape(B, S, D)
```

```python
import functools
import math
from typing import NamedTuple

import jax
import jax.numpy as jnp
from jax import lax
from jax.experimental import pallas as pl
from jax.experimental.pallas import tpu as pltpu

D_MODEL = 1024
D_FF = 2816
D_A = 2 * D_MODEL
CHUNK = 128
SGU_GROUPS = 8
SGU_GROUP_DIM = D_A // SGU_GROUPS
SB_HEADS = 16
SB_HEAD_DIM = 64
SB_KV_HEADS = 4
SB_GROUP = SB_HEADS // SB_KV_HEADS
FFN_RES_SCALE = 0.5
NORM_EPS = 1e-6
LN_EPS = 1e-5
LOG2E = 1.4426950408889634
SOFTPLUS_CLAMP = 64.0
DEAD_RUN = -160.0
FINISHED_RUN = -1e30

VMEM_LIMIT_BYTES = 56 * 1024 * 1024

BF16_SUBLANE_TILE = 16
LANES = 128

TOKEN_TILE = 512
FFN_TILE = 1024
FFN_CHUNK = 256
ATT_BLOCK = 256
ATT_Q_SUB = 8

BF16 = jnp.bfloat16
F32 = jnp.float32


def _const_spec(shape):
    nd = len(shape)
    return pl.BlockSpec(shape, lambda *_: (0,) * nd, pipeline_mode=pl.Buffered(1))


def _rms_norm(x, g):
    return x * lax.rsqrt(jnp.mean(x * x, axis=-1, keepdims=True) + NORM_EPS) * g


def _gelu_tanh(x):
    c1 = math.sqrt(2.0 / math.pi)
    c2 = c1 * 0.044715
    return x * (0.5 + 0.5 * jnp.tanh(x * (c1 + c2 * (x * x))))


def _params(n_grid_dims, semantics="parallel"):
    return pltpu.CompilerParams(
        dimension_semantics=(semantics,) * n_grid_dims,
        vmem_limit_bytes=VMEM_LIMIT_BYTES)


class _Cast(NamedTuple):
    src: jax.Array
    lead: tuple
    rows: int
    cols: int


def _cast_io(casts, steps):
    in_specs, out_specs, out_shape, args = [], [], [], []
    for c in casts:
        n_slabs = max(n for n in range(1, steps + 1)
                      if c.rows % n == 0 and (c.rows // n) % BF16_SUBLANE_TILE == 0)
        slab = c.rows // n_slabs
        lead = c.lead
        in_specs.append(pl.BlockSpec(
            (None,) * len(lead) + (slab, c.cols),
            lambda i, lead=lead, last=n_slabs - 1: lead + (jnp.minimum(i, last), 0)))
        out_specs.append(pl.BlockSpec(
            (slab, c.cols), lambda i, last=n_slabs - 1: (jnp.minimum(i, last), 0)))
        out_shape.append(jax.ShapeDtypeStruct((c.rows, c.cols), BF16))
        args.append(c.src)
    return in_specs, out_specs, out_shape, args


def _do_casts(cast_in_refs, cast_out_refs):
    for src, dst in zip(cast_in_refs, cast_out_refs):
        dst[...] = src[...].astype(BF16)


class _FfnPlan(NamedTuple):
    attn_in: bool
    final_norm: bool
    proj_outs: int
    proj_scale: float
    n_casts: int


def _ffn_kernel(*refs, plan):
    it = iter(refs)
    x_ref = next(it)
    if plan.attn_in:
        o_ref, wo_ref = next(it), next(it)
    g_ref, win_ref, wout_ref = next(it), next(it), next(it)
    if plan.final_norm:
        fg_ref = next(it)
    if plan.proj_outs:
        pg_ref, pw_ref = next(it), next(it)
    cast_in_refs = [next(it) for _ in range(plan.n_casts)]
    y_ref = next(it)
    proj_refs = [next(it) for _ in range(plan.proj_outs)]
    _do_casts(cast_in_refs, list(it))

    x = x_ref[...]
    if plan.attn_in:
        x = x + jnp.dot(o_ref[...], wo_ref[...], preferred_element_type=F32)
    h = _rms_norm(x, g_ref[...]).astype(BF16)
    acc = jnp.zeros(x.shape, F32)
    for c in range(D_FF // FFN_CHUNK):
        lo = c * FFN_CHUNK
        gate = jnp.dot(h, win_ref[:, lo:lo + FFN_CHUNK], preferred_element_type=F32)
        up = jnp.dot(h, win_ref[:, D_FF + lo:D_FF + lo + FFN_CHUNK], preferred_element_type=F32)
        act = (gate * jax.nn.sigmoid(gate) * up).astype(BF16)
        acc = acc + jnp.dot(act, wout_ref[lo:lo + FFN_CHUNK, :], preferred_element_type=F32)
    y = x + FFN_RES_SCALE * acc
    if plan.proj_outs:
        p = jnp.dot(_rms_norm(y, pg_ref[...]).astype(BF16), pw_ref[...],
                    preferred_element_type=F32)
        if plan.proj_scale != 1.0:
            p = p * plan.proj_scale
        p = p.astype(BF16)
        heads = proj_refs[0].shape[1]
        for oi, p_ref in enumerate(proj_refs):
            for hd in range(heads):
                c0 = (oi * heads + hd) * SB_HEAD_DIM
                p_ref[0, hd] = p[:, c0:c0 + SB_HEAD_DIM]
    if plan.final_norm:
        y = _rms_norm(y, fg_ref[...])
    y_ref[...] = y


def _ffn(x2, seq_len, gain, w_in, w_out, name, *, attn=None, final_g=None, proj=None, casts=()):
    T = x2.shape[0]
    plan = _FfnPlan(attn is not None, final_g is not None,
                    0 if proj is None else proj[2], 1.0 if proj is None else proj[3], len(casts))
    tm = TOKEN_TILE if plan.proj_outs > 1 else FFN_TILE
    row = pl.BlockSpec((tm, D_MODEL), lambda i: (i, 0))

    in_specs, args = [row], [x2]
    if plan.attn_in:
        o2, w_o = attn
        in_specs += [row, _const_spec(w_o.shape)]
        args += [o2, w_o]
    in_specs += [_const_spec((1, D_MODEL)), _const_spec(w_in.shape), _const_spec(w_out.shape)]
    args += [gain.reshape(1, D_MODEL), w_in, w_out]
    if plan.final_norm:
        in_specs.append(_const_spec((1, D_MODEL)))
        args.append(final_g.reshape(1, D_MODEL))
    out_specs, out_shape = [row], [jax.ShapeDtypeStruct((T, D_MODEL), F32)]
    if plan.proj_outs:
        p_g, p_w = proj[0], proj[1]
        in_specs += [_const_spec((1, D_MODEL)), _const_spec(p_w.shape)]
        args += [p_g.reshape(1, D_MODEL), p_w]
        heads = p_w.shape[1] // (plan.proj_outs * SB_HEAD_DIM)
        tiles_per_seq = seq_len // tm
        out_specs += [pl.BlockSpec((1, heads, tm, SB_HEAD_DIM),
                                   lambda i: (i // tiles_per_seq, 0, i % tiles_per_seq, 0))
                      ] * plan.proj_outs
        out_shape += [jax.ShapeDtypeStruct((T // seq_len, heads, seq_len, SB_HEAD_DIM), BF16)
                      ] * plan.proj_outs
    c_in, c_out, c_shape, c_args = _cast_io(casts, T // tm)
    return pl.pallas_call(
        functools.partial(_ffn_kernel, plan=plan),
        grid=(T // tm,),
        in_specs=in_specs + c_in,
        out_specs=out_specs + c_out,
        out_shape=out_shape + c_shape,
        compiler_params=_params(1, "arbitrary"),
        name=name,
    )(*args, *c_args)


def _sgu_kernel(x_ref, g_ref, win_ref, lng_ref, lnb_ref, ws_ref, bs_ref, wout_ref, *rest):
    n_casts = (len(rest) - 1) // 2
    o_ref = rest[n_casts]
    _do_casts(rest[:n_casts], rest[n_casts + 1:])

    x = x_ref[...]
    tm = x.shape[0]
    h = _rms_norm(x, g_ref[...]).astype(BF16)
    def in_proj(c0):
        return jnp.dot(h, win_ref[:, c0:c0 + SGU_GROUP_DIM], preferred_element_type=F32)

    v = jnp.concatenate([_gelu_tanh(in_proj(D_A + gi * SGU_GROUP_DIM))
                         for gi in range(SGU_GROUPS)], axis=1)
    u_raw = [in_proj(gi * SGU_GROUP_DIM) for gi in range(SGU_GROUPS)]
    mu = jnp.mean(v, axis=-1, keepdims=True)
    vc = v - mu
    var = jnp.mean(vc * vc, axis=-1, keepdims=True)
    vn = (vc * lax.rsqrt(var + LN_EPS) * lng_ref[...] + lnb_ref[...]).astype(BF16)

    row = lax.broadcasted_iota(jnp.int32, (CHUNK, CHUNK), 0)
    col = lax.broadcasted_iota(jnp.int32, (CHUNK, CHUNK), 1)
    causal = row >= col
    bs = bs_ref[...]

    def spatial(gi):
        ws = jnp.where(causal, ws_ref[gi], 0.0).astype(BF16)
        bias = bs[:, gi:gi + 1]
        c0 = gi * SGU_GROUP_DIM
        return jnp.concatenate(
            [jnp.dot(ws, vn[n * CHUNK:(n + 1) * CHUNK, c0:c0 + SGU_GROUP_DIM],
                     preferred_element_type=F32) + bias for n in range(tm // CHUNK)], axis=0)

    acc = jnp.zeros(x.shape, F32)
    vs_next = spatial(0)
    for gi in range(SGU_GROUPS):
        vs = vs_next
        if gi + 1 < SGU_GROUPS:
            vs_next = spatial(gi + 1)
        gated = (_gelu_tanh(u_raw[gi]) * vs).astype(BF16)
        acc = acc + jnp.dot(gated, wout_ref[gi * SGU_GROUP_DIM:(gi + 1) * SGU_GROUP_DIM, :],
                            preferred_element_type=F32)
    o_ref[...] = x + acc


def _sgu(x2, g, w_in, ln_g, ln_b, w_s, b_s, w_out, casts=()):
    T = x2.shape[0]
    tm = TOKEN_TILE
    row = pl.BlockSpec((tm, D_MODEL), lambda i: (i, 0))
    c_in, c_out, c_shape, c_args = _cast_io(casts, T // tm)
    return pl.pallas_call(
        _sgu_kernel,
        grid=(T // tm,),
        in_specs=[row, _const_spec((1, D_MODEL)), _const_spec(w_in.shape),
                  _const_spec((1, D_A)), _const_spec((1, D_A)),
                  _const_spec(w_s.shape), _const_spec((CHUNK, SGU_GROUPS)),
                  _const_spec(w_out.shape)] + c_in,
        out_specs=[row] + c_out,
        out_shape=[jax.ShapeDtypeStruct((T, D_MODEL), F32)] + c_shape,
        compiler_params=_params(1, "arbitrary"),
        name="sgu",
    )(x2, g.reshape(1, D_MODEL), w_in, ln_g.reshape(1, D_A), ln_b.reshape(1, D_A),
      w_s, b_s.T, w_out, *c_args)


def _attn_kernel(q_ref, k_ref, v_ref, suffix_ref, o_ref, acc_ref, run_ref):
    qt = pl.program_id(2)
    blk = ATT_BLOCK
    nsub = ATT_Q_SUB
    row = lax.broadcasted_iota(jnp.int32, (blk, blk), 0)
    col = lax.broadcasted_iota(jnp.int32, (blk, blk), 1)
    suffix = suffix_ref[...]
    strictly_earlier = col < row

    def load_kv(j):
        rows = pl.ds(pl.multiple_of(jnp.maximum(j, 0) * blk, blk), blk)
        return k_ref[0, 0, rows, :], v_ref[0, 0, rows, :]

    def finished_after(j):
        return jnp.where(j >= 1, 0.0, FINISHED_RUN).astype(F32)

    def scores(unit):
        sub, hd, (kb, _), diag, retire = unit
        idx = sub * SB_GROUP + hd
        z = lax.dot_general(q_ref[0, hd, sub * blk:(sub + 1) * blk, :], kb,
                            (((1,), (1,)), ((), ())), preferred_element_type=F32)
        sp = jnp.maximum(z, jnp.log2(1.0 + jnp.exp2(jnp.minimum(z, SOFTPLUS_CLAMP))))
        if diag:
            sp = jnp.where(strictly_earlier, sp, 0.0)
        run = 0.0 if diag else run_ref[idx]
        new_run = run - jnp.broadcast_to(jnp.sum(sp, axis=-1, keepdims=True), (blk, LANES))
        if retire is not None:
            new_run = new_run + retire
        run_ref[idx] = new_run
        w = z - sp
        if not diag:
            w = w + jnp.concatenate([run, run], axis=1)
        return sp.astype(BF16), w, new_run

    def weights(unit, st):
        sp, w = st
        a = jnp.exp2(w + jnp.dot(sp, suffix, preferred_element_type=F32))
        if unit[3]:
            a = jnp.where(strictly_earlier, a, 0.0)
        return a.astype(BF16)

    def accumulate(unit, a):
        sub, hd, (_, vb), diag, _ = unit
        idx = sub * SB_GROUP + hd
        pv = jnp.dot(a, vb, preferred_element_type=F32)
        if diag:
            acc_ref[idx] = pv
        else:
            acc_ref[idx] += pv

    def run_units(units, tracked):
        n = len(units)
        st, a = {}, {}
        max_run = None
        for step in range(n + 2):
            if step < n:
                sp, w, new_run = scores(units[step])
                st[step] = (sp, w)
                if step in tracked:
                    max_run = new_run if max_run is None else jnp.maximum(max_run, new_run)
            if 0 <= step - 1 < n:
                a[step - 1] = weights(units[step - 1], st.pop(step - 1))
            if step - 2 >= 0:
                accumulate(units[step - 2], a.pop(step - 2))
        return max_run

    def alive(max_run):
        return (jnp.max(max_run) > DEAD_RUN).astype(jnp.int32)

    first = qt * nsub
    units, tracked = [], set()
    for off in range(nsub - 1, -2, -1):
        j = first + off
        kv = load_kv(j)
        retire = finished_after(j) if off <= 0 else None
        for sub in (off, off + 1):
            if 0 <= sub < nsub:
                for hd in range(SB_GROUP):
                    if sub == off + 1:
                        tracked.add(len(units))
                    units.append((sub, hd, kv, sub == off, retire))
    head_max = run_units(units, tracked)

    n_rounds = first + nsub - 2

    def more_rounds(carry):
        i, live = carry
        return jnp.logical_and(i < n_rounds, live > 0)

    def one_round(carry):
        i, _ = carry
        units = []
        for sub in range(nsub):
            j = first + sub - 2 - i
            kv = load_kv(j)
            retire = finished_after(j)
            units += [(sub, hd, kv, False, retire) for hd in range(SB_GROUP)]
        return i + 1, alive(run_units(units, set(range(len(units)))))

    lax.while_loop(more_rounds, one_round, (jnp.int32(0), alive(head_max)))

    for sub in range(nsub):
        o_ref[0, sub * blk:(sub + 1) * blk, :] = jnp.concatenate(
            [acc_ref[sub * SB_GROUP + hd] for hd in range(SB_GROUP)], axis=1).astype(o_ref.dtype)


def _attention(q, k, v):
    B, _, S, _ = q.shape
    tq = ATT_BLOCK * ATT_Q_SUB
    kv_spec = pl.BlockSpec((1, 1, S, SB_HEAD_DIM), lambda b, kh, i: (b, kh, 0, 0))
    return pl.pallas_call(
        _attn_kernel,
        grid=(B, SB_KV_HEADS, S // tq),
        in_specs=[pl.BlockSpec((1, SB_GROUP, tq, SB_HEAD_DIM), lambda b, kh, i: (b, kh, i, 0)),
                  kv_spec, kv_spec, _const_spec((ATT_BLOCK, ATT_BLOCK))],
        out_specs=pl.BlockSpec((1, tq, SB_GROUP * SB_HEAD_DIM), lambda b, kh, i: (b, i, kh)),
        out_shape=jax.ShapeDtypeStruct((B, S, SB_HEADS * SB_HEAD_DIM), BF16),
        scratch_shapes=[pltpu.VMEM((ATT_Q_SUB * SB_GROUP, ATT_BLOCK, SB_HEAD_DIM), F32),
                        pltpu.VMEM((ATT_Q_SUB * SB_GROUP, ATT_BLOCK, LANES), F32)],
        compiler_params=_params(3),
        name="sb_attention",
    )(q, k, v, jnp.tril(jnp.full((ATT_BLOCK, ATT_BLOCK), -1.0, BF16), k=-1))


def kernel(x, ffn_norm, ffn_w_in, ffn_w_out, mix_norm, a_w_in, a_ln_g, a_ln_b, a_w_s, a_b_s,
           a_w_out, kv_norm, w_kv, b_w_q, b_w_o, final_norm):
    B, S, D = x.shape
    depth = ffn_norm.shape[0]
    n_a = a_w_in.shape[0]
    assert 1 <= n_a < depth, "shared K/V are produced by the last mixer-A layer"
    q_scale = LOG2E / math.sqrt(SB_HEAD_DIM)

    def ffn_casts(layer, half):
        return [_Cast(ffn_w_in, (layer, half), D, 2 * D_FF),
                _Cast(ffn_w_out, (layer, half), D_FF, D)]

    def plain(w):
        return _Cast(w, (), w.shape[0], w.shape[1])

    w_in, w_out = ffn_w_in[0, 0].astype(BF16), ffn_w_out[0, 0].astype(BF16)
    x2 = x.reshape(B * S, D)
    k = v = None
    for i in range(depth):
        if i < n_a:
            x2, a_in, a_out = _ffn(x2, S, ffn_norm[i, 0], w_in, w_out, "ffn_l%d_h0" % i,
                                   casts=[plain(a_w_in[i]), plain(a_w_out[i])])
            outs = _sgu(x2, mix_norm[i], a_in, a_ln_g[i], a_ln_b[i], a_w_s[i], a_b_s[i], a_out,
                        casts=ffn_casts(i, 1) + ([plain(w_kv)] if i == n_a - 1 else []))
            x2, w_in, w_out = outs[:3]
            if i == n_a - 1:
                x2, k, v, w_in, w_out, w_q = _ffn(
                    x2, S, ffn_norm[i, 1], w_in, w_out, "ffn_l%d_h1" % i,
                    proj=(kv_norm, outs[3], 2, 1.0),
                    casts=ffn_casts(i + 1, 0) + [plain(b_w_q[0])])
            else:
                x2, w_in, w_out = _ffn(x2, S, ffn_norm[i, 1], w_in, w_out, "ffn_l%d_h1" % i,
                                       casts=ffn_casts(i + 1, 0))
        else:
            j = i - n_a
            x2, q, w_in, w_out, w_o = _ffn(
                x2, S, ffn_norm[i, 0], w_in, w_out, "ffn_l%d_h0" % i,
                proj=(mix_norm[i], w_q, 1, q_scale),
                casts=ffn_casts(i, 1) + [plain(b_w_o[j])])
            o = _attention(q, k, v)
            last = i == depth - 1
            casts = [] if last else ffn_casts(i + 1, 0) + [plain(b_w_q[j + 1])]
            outs = _ffn(x2, S, ffn_norm[i, 1], w_in, w_out, "ffn_l%d_h1" % i,
                        attn=(o.reshape(B * S, D), w_o),
                        final_g=final_norm if last else None, casts=casts)
            x2 = outs[0]
            if not last:
                w_in, w_out, w_q = outs[1:4]
    return x2.reshape(B, S, D)
```
